```python
import jax, jax.numpy as jnp
from jax import lax
import numpy as np

D_MODEL = 4096
BATCH = 2
SEQ = 4096
DEPTH = 2
DEC_BATCH = 32
DEC_SEQ = 32
PAST_LEN = 2048

CHUNK = 64
N_MIXERS = 2
N_A_LAYERS = (DEPTH + 1) // 2
N_B_LAYERS = DEPTH // 2
HG_HEAD_K = 128
HG_HEADS = D_MODEL // HG_HEAD_K
HG_HEAD_V = D_MODEL // HG_HEADS
HG_DK = HG_HEADS * HG_HEAD_K
HG_DV = HG_HEADS * HG_HEAD_V
RW_HEAD = 64
RW_HEADS = D_MODEL // RW_HEAD
RW_DECAY_LORA = 128
RW_AAA_LORA = 128
RW_GATE_LORA = 480
D_FF = 11008
CONV_W = 3
PLE_DIM = 256
RMS_EPS = 1e-6
GN_EPS = 64e-5

kernel_name = 'hgrn2_rwkv7_convffn_stream_step'


def rmsnorm(x, g):
    xf = x.astype(jnp.float32)
    y = xf * lax.rsqrt(jnp.mean(xf * xf, axis=-1, keepdims=True) + RMS_EPS)
    return (y * g.astype(jnp.float32)).astype(x.dtype)


def to_chunks(a, c):
    b, t, h, d = a.shape
    return a.reshape(b, t // c, c, h, d).transpose(1, 0, 3, 2, 4)


def hgrn2_chunk_scan(q, k, v, logf, s0):
    b, t, h, _ = q.shape
    c = min(CHUNK, t)
    mask = jnp.tril(jnp.ones((c, c), dtype=bool))[None, None, :, :, None]

    def step(S, blk):
        qb, kb, vb, gb = blk
        G = jnp.cumsum(gb, axis=2)
        diff = G[:, :, :, None, :] - G[:, :, None, :, :]
        decay_ts = jnp.exp(jnp.where(mask, diff, -jnp.inf))
        A = jnp.einsum('bhtk,bhsk,bhtsk->bhts', qb, kb, decay_ts)
        o = (jnp.einsum('bhts,bhsv->bhtv', A, vb)
             + jnp.einsum('bhtk,bhkv->bhtv', qb * jnp.exp(G), S))
        G_end = G[:, :, -1, :]
        S = (S * jnp.exp(G_end)[..., None]
             + jnp.einsum('bhsk,bhsv->bhkv', kb * jnp.exp(G_end[:, :, None, :] - G), vb))
        return S, o

    S, o = lax.scan(step, s0, (to_chunks(q, c), to_chunks(k, c), to_chunks(v, c), to_chunks(logf, c)))
    o = o.transpose(1, 0, 3, 2, 4).reshape(b, t, h, v.shape[-1])
    return o, S


def hgrn2_mixer(xn, s0, w_in, lb, g_norm, w_o):
    b, t, _ = xn.shape
    q, f, i, g = jnp.split(xn @ w_in, [HG_DK, 2 * HG_DK, 2 * HG_DK + HG_DV], axis=-1)
    lbf = lb.astype(jnp.float32)
    fg = lbf + (1.0 - lbf) * jax.nn.sigmoid(f.astype(jnp.float32))
    logf = jnp.log(fg)
    k = 1.0 - fg
    qf = jax.nn.silu(q.astype(jnp.float32))
    hk = lambda a: a.reshape(b, t, HG_HEADS, HG_HEAD_K)
    o, s_new = hgrn2_chunk_scan(hk(qf), hk(k), i.astype(jnp.float32).reshape(b, t, HG_HEADS, HG_HEAD_V),
                                hk(logf), s0.astype(jnp.float32))
    o = rmsnorm(o, g_norm).reshape(b, t, HG_DV)
    o = (o * jax.nn.sigmoid(g.astype(jnp.float32))).astype(xn.dtype)
    return o @ w_o, s_new


def rwkv7_scan(r, w, k, v, kk, a, s0):
    def step(S, inp):
        r_, w_, k_, v_, kk_, a_ = inp
        s_kk = jnp.einsum('bhij,bhj->bhi', S, kk_)
        S = (S * w_[:, :, None, :] - s_kk[..., None] * (kk_ * a_)[:, :, None, :]
             + v_[..., None] * k_[:, :, None, :])
        return S, jnp.einsum('bhij,bhj->bhi', S, r_)

    tm = lambda z: jnp.swapaxes(z, 0, 1)
    S, y = lax.scan(step, s0, (tm(r), tm(w), tm(k), tm(v), tm(kk), tm(a)))
    return tm(y), S


def rwkv7_mixer(xn, shift0, s0, mu, w_rkv, w0, w1, w2, a0, a1, a2, g1, g2,
                k_k, k_a, r_k, lnx_w, lnx_b, w_o):
    b, t, d = xn.shape
    x_prev = jnp.concatenate([shift0[:, None, :].astype(xn.dtype), xn[:, :-1]], axis=1)
    xx = x_prev - xn
    xr, xw, xk, xv, xa, xg = (xn + xx * mu[j] for j in range(6))
    wr, wk, wv = jnp.split(w_rkv, 3, axis=1)
    r, k, v = xr @ wr, xk @ wk, xv @ wv
    w_log = -jax.nn.softplus(-(w0 + jnp.tanh(xw @ w1) @ w2).astype(jnp.float32)) - 0.5
    decay = jnp.exp(-jnp.exp(w_log))
    a = jax.nn.sigmoid((a0 + (xa @ a1) @ a2).astype(jnp.float32))
    g = jax.nn.sigmoid(xg @ g1) @ g2
    hs = lambda z: z.astype(jnp.float32).reshape(b, t, RW_HEADS, RW_HEAD)
    r_h, k_h, v_h, a_h, w_h = hs(r), hs(k), hs(v), hs(a), hs(decay)
    kk = k_h * hs(k_k)[0, 0] if False else k_h * k_k.astype(jnp.float32).reshape(RW_HEADS, RW_HEAD)
    kk = kk / jnp.maximum(jnp.sqrt(jnp.sum(kk * kk, axis=-1, keepdims=True)), 1e-12)
    k_h = k_h * (1.0 + (a_h - 1.0) * k_a.astype(jnp.float32).reshape(RW_HEADS, RW_HEAD))
    y, s_new = rwkv7_scan(r_h, w_h, k_h, v_h, kk, a_h, s0.astype(jnp.float32))
    mean = jnp.mean(y, axis=-1, keepdims=True)
    var = jnp.mean(jnp.square(y - mean), axis=-1, keepdims=True)
    y = ((y - mean) * lax.rsqrt(var + GN_EPS)).reshape(b, t, d) * lnx_w + lnx_b
    bonus = jnp.sum(r_h * k_h * r_k.astype(jnp.float32), axis=-1, keepdims=True) * v_h
    y = y + bonus.reshape(b, t, d)
    return (y * g).astype(xn.dtype) @ w_o, xn[:, -1], s_new


def conv_ffn(xn, c0, w_up, conv_w, conv_b, w_down):
    t = xn.shape[1]
    hg, hu = jnp.split(xn @ w_up, 2, axis=-1)
    hpad = jnp.concatenate([c0.astype(hg.dtype), hg], axis=1)
    hc = conv_b + sum(conv_w[j] * hpad[:, j:j + t] for j in range(CONV_W))
    return (jax.nn.gelu(hc, approximate=False) * hu) @ w_down, hpad[:, t:]


def run_trunk(h, p, s_hg, s_rw, s_shift, s_conv,
              norm_mix, norm_ffn, norm_ple, norm_final,
              hg_w_in, hg_lb_logits, hg_gnorm, hg_w_o,
              rw_mu, rw_w_rkv, rw_w0, rw_w1, rw_w2, rw_a0, rw_a1, rw_a2,
              rw_g1, rw_g2, rw_k_k, rw_k_a, rw_r_k, rw_lnx_w, rw_lnx_b, rw_w_o,
              ffn_w_up, ffn_conv_w, ffn_conv_b, ffn_w_down,
              ple_w_proj, ple_w_gate):
    lb_all = jnp.cumsum(jax.nn.softmax(hg_lb_logits.astype(jnp.float32), axis=0), axis=0)
    new_hg, new_rw, new_shift, new_conv = [], [], [], []
    for i in range(DEPTH):
        j = i // N_MIXERS
        xn = rmsnorm(h, norm_mix[i])
        if i % N_MIXERS == 0:
            mix, s = hgrn2_mixer(xn, s_hg[j], hg_w_in[j], lb_all[j], hg_gnorm[j], hg_w_o[j])
            new_hg.append(s)
        else:
            mix, sh, s = rwkv7_mixer(xn, s_shift[j], s_rw[j], rw_mu[j], rw_w_rkv[j], rw_w0[j], rw_w1[j],
                                     rw_w2[j], rw_a0[j], rw_a1[j], rw_a2[j], rw_g1[j], rw_g2[j],
                                     rw_k_k[j], rw_k_a[j], rw_r_k[j], rw_lnx_w[j], rw_lnx_b[j], rw_w_o[j])
            new_rw.append(s)
            new_shift.append(sh)
        h = h + mix
        f, c = conv_ffn(rmsnorm(h, norm_ffn[i]), s_conv[i], ffn_w_up[i], ffn_conv_w[i], ffn_conv_b[i], ffn_w_down[i])
        h = h + f
        new_conv.append(c)
        gate = jax.nn.sigmoid(rmsnorm(h, norm_ple[i]) @ ple_w_gate[i])
        h = h + gate * (p[i] @ ple_w_proj[i])
    y = rmsnorm(h, norm_final)
    return y, jnp.stack(new_hg), jnp.stack(new_rw), jnp.stack(new_shift), jnp.stack(new_conv)


def setup_inputs(seed: int = 0) -> dict:
    key = jax.random.key(seed)
    ks = jax.random.split(key, 38)
    nrm = lambda i, shape, scale: scale * jax.random.normal(ks[i], shape, jnp.float32)
    D = D_MODEL
    return {
        'x_prompt': nrm(0, (BATCH, SEQ, D), 1.0),
        'x_sample': nrm(1, (DEC_BATCH, DEC_SEQ, D), 1.0),
        'p_prompt': nrm(2, (DEPTH, BATCH, SEQ, PLE_DIM), 1.0),
        'p_sample': nrm(3, (DEPTH, DEC_BATCH, DEC_SEQ, PLE_DIM), 1.0),
        'state_hgrn': nrm(4, (N_A_LAYERS, DEC_BATCH, HG_HEADS, HG_HEAD_K, HG_HEAD_V), 0.5),
        'state_rwkv': nrm(5, (N_B_LAYERS, DEC_BATCH, RW_HEADS, RW_HEAD, RW_HEAD), 0.1),
        'state_shift': nrm(6, (N_B_LAYERS, DEC_BATCH, D), 1.0),
        'state_ffn_conv': nrm(7, (DEPTH, DEC_BATCH, CONV_W - 1, D_FF), 1.0),
        'norm_mix': 1.0 + nrm(8, (DEPTH, D), 0.02),
        'norm_ffn': 1.0 + nrm(9, (DEPTH, D), 0.02),
        'norm_ple': 1.0 + nrm(10, (DEPTH, D), 0.02),
        'norm_final': 1.0 + nrm(11, (D,), 0.02),
        'hg_w_in': nrm(12, (N_A_LAYERS, D, 3 * HG_DK + HG_DV), D ** -0.5),
        'hg_lb_logits': nrm(13, (N_A_LAYERS + 1, HG_DK), 0.5),
        'hg_gnorm': 1.0 + nrm(14, (N_A_LAYERS, HG_HEAD_V), 0.02),
        'hg_w_o': nrm(15, (N_A_LAYERS, HG_DV, D), HG_DV ** -0.5),
        'rw_mu': jax.random.uniform(ks[16], (N_B_LAYERS, 6, D), jnp.float32),
        'rw_w_rkv': nrm(17, (N_B_LAYERS, D, 3 * D), D ** -0.5),
        'rw_w0': -1.0 + nrm(18, (N_B_LAYERS, D), 0.5),
        'rw_w1': nrm(19, (N_B_LAYERS, D, RW_DECAY_LORA), D ** -0.5),
        'rw_w2': nrm(20, (N_B_LAYERS, RW_DECAY_LORA, D), 0.5 * RW_DECAY_LORA ** -0.5),
        'rw_a0': nrm(21, (N_B_LAYERS, D), 0.1),
        'rw_a1': nrm(22, (N_B_LAYERS, D, RW_AAA_LORA), D ** -0.5),
        'rw_a2': nrm(23, (N_B_LAYERS, RW_AAA_LORA, D), RW_AAA_LORA ** -0.5),
        'rw_g1': nrm(24, (N_B_LAYERS, D, RW_GATE_LORA), D ** -0.5),
        'rw_g2': nrm(25, (N_B_LAYERS, RW_GATE_LORA, D), RW_GATE_LORA ** -0.5),
        'rw_k_k': 0.85 + nrm(26, (N_B_LAYERS, D), 0.05),
        'rw_k_a': 1.0 + nrm(27, (N_B_LAYERS, D), 0.05),
        'rw_r_k': nrm(28, (N_B_LAYERS, RW_HEADS, RW_HEAD), 0.1),
        'rw_lnx_w': 1.0 + nrm(29, (N_B_LAYERS, D), 0.02),
        'rw_lnx_b': nrm(30, (N_B_LAYERS, D), 0.02),
        'rw_w_o': nrm(31, (N_B_LAYERS, D, D), D ** -0.5),
        'ffn_w_up': nrm(32, (DEPTH, D, 2 * D_FF), D ** -0.5),
        'ffn_conv_w': nrm(33, (DEPTH, CONV_W, D_FF), CONV_W ** -0.5),
        'ffn_conv_b': nrm(34, (DEPTH, D_FF), 0.02),
        'ffn_w_down': nrm(35, (DEPTH, D_FF, D), D_FF ** -0.5),
        'ple_w_proj': nrm(36, (DEPTH, PLE_DIM, D), PLE_DIM ** -0.5),
        'ple_w_gate': nrm(37, (DEPTH, D, D), D ** -0.5),
    }


def reference(x_prompt, x_sample, p_prompt, p_sample, state_hgrn, state_rwkv, state_shift, state_ffn_conv,
              norm_mix, norm_ffn, norm_ple, norm_final,
              hg_w_in, hg_lb_logits, hg_gnorm, hg_w_o,
              rw_mu, rw_w_rkv, rw_w0, rw_w1, rw_w2, rw_a0, rw_a1, rw_a2,
              rw_g1, rw_g2, rw_k_k, rw_k_a, rw_r_k, rw_lnx_w, rw_lnx_b, rw_w_o,
              ffn_w_up, ffn_conv_w, ffn_conv_b, ffn_w_down,
              ple_w_proj, ple_w_gate):
    weights = (norm_mix, norm_ffn, norm_ple, norm_final,
               hg_w_in, hg_lb_logits, hg_gnorm, hg_w_o,
               rw_mu, rw_w_rkv, rw_w0, rw_w1, rw_w2, rw_a0, rw_a1, rw_a2,
               rw_g1, rw_g2, rw_k_k, rw_k_a, rw_r_k, rw_lnx_w, rw_lnx_b, rw_w_o,
               ffn_w_up, ffn_conv_w, ffn_conv_b, ffn_w_down,
               ple_w_proj, ple_w_gate)
    b = x_prompt.shape[0]
    z_hg = jnp.zeros((N_A_LAYERS, b, HG_HEADS, HG_HEAD_K, HG_HEAD_V), jnp.float32)
    z_rw = jnp.zeros((N_B_LAYERS, b, RW_HEADS, RW_HEAD, RW_HEAD), jnp.float32)
    z_sh = jnp.zeros((N_B_LAYERS, b, D_MODEL), x_prompt.dtype)
    z_cv = jnp.zeros((DEPTH, b, CONV_W - 1, D_FF), x_prompt.dtype)
    y_prompt, hg_p, rw_p, sh_p, cv_p = run_trunk(x_prompt, p_prompt, z_hg, z_rw, z_sh, z_cv, *weights)
    y_sample, hg_s, rw_s, sh_s, cv_s = run_trunk(x_sample, p_sample, state_hgrn, state_rwkv, state_shift,
                                                 state_ffn_conv, *weights)
    return (y_prompt, y_sample, hg_p, rw_p, sh_p, cv_p, hg_s, rw_s, sh_s, cv_s)
```

```python
import functools

import numpy as np
import jax
import jax.numpy as jnp
from jax import lax
from jax.experimental import pallas as pl
from jax.experimental.pallas import tpu as pltpu

f32 = jnp.float32
bf16 = jnp.bfloat16

RMS_EPS = 1e-6
GN_EPS = 64e-5
HG_HEAD = 128
RW_HEAD = 64
RW_SUB = 16
CHUNK = 64
VMEM_LIMIT = 50 * 1024 * 1024
HI = lax.Precision.HIGHEST


def _pick(n, prefs):
    for p in prefs:
        if n % p == 0:
            return p
    return n


def _params(sem):
    return pltpu.CompilerParams(dimension_semantics=sem, vmem_limit_bytes=VMEM_LIMIT)


def _nn(a, b):
    return jnp.dot(a.astype(bf16), b.astype(bf16), preferred_element_type=f32)


def _nt(a, b):
    return lax.dot_general(a.astype(bf16), b.astype(bf16), (((1,), (1,)), ((), ())), preferred_element_type=f32)


def _tn(a, b):
    return lax.dot_general(a.astype(bf16), b.astype(bf16), (((0,), (0,)), ((), ())), preferred_element_type=f32)


def _split(a):
    hi = a.astype(bf16)
    lo = (a - hi.astype(f32)).astype(bf16)
    return hi, lo


def _nn3(a, b):
    ah, al = _split(a)
    bh, bl = _split(b)
    d = functools.partial(jnp.dot, preferred_element_type=f32)
    return d(ah, bh) + (d(ah, bl) + d(al, bh))


def _rms_body(x_ref, g_ref, o_ref):
    x = x_ref[...]
    ms = jnp.mean(x * x, axis=-1, keepdims=True)
    o_ref[...] = (x * lax.rsqrt(ms + RMS_EPS) * g_ref[...]).astype(o_ref.dtype)


def _rmsnorm(x, g, out_dtype):
    m, d = x.shape
    tm = _pick(m, (256, 128, 64, 32, 16, 8))
    return pl.pallas_call(
        _rms_body,
        grid=(m // tm,),
        in_specs=[pl.BlockSpec((tm, d), lambda i: (i, 0)), pl.BlockSpec((1, d), lambda i: (0, 0))],
        out_specs=pl.BlockSpec((tm, d), lambda i: (i, 0)),
        out_shape=jax.ShapeDtypeStruct((m, d), out_dtype),
        compiler_params=_params(("parallel",)),
        name="rmsnorm",
    )(x, g.reshape(1, d))


def _mm_body(*refs, nd, nt, nr, epi):
    dots = refs[: 2 * nd]
    tiles = refs[2 * nd: 2 * nd + nt]
    rows = refs[2 * nd + nt: 2 * nd + nt + nr]
    o_ref = refs[-1]
    accs = [jnp.dot(dots[2 * i][...], dots[2 * i + 1][...], preferred_element_type=f32) for i in range(nd)]
    o_ref[...] = epi(accs, [t[...] for t in tiles], [r[...] for r in rows]).astype(o_ref.dtype)


def _mm(dots, epi, out_dtype, tiles=(), rows=(), tm=1024, tn=512, name="mm"):
    m = dots[0][0].shape[0]
    n = dots[0][1].shape[1]
    tm = _pick(m, (tm, 512, 256, 128, 64, 32, 16, 8))
    tn = _pick(n, (tn, 256, 128))
    ins, specs = [], []
    for x, w in dots:
        k = x.shape[1]
        ins += [x, w]
        specs += [pl.BlockSpec((tm, k), lambda i, j: (i, 0)), pl.BlockSpec((k, tn), lambda i, j: (0, j))]
    for t in tiles:
        ins.append(t)
        specs.append(pl.BlockSpec((tm, tn), lambda i, j: (i, j)))
    for r in rows:
        ins.append(r.reshape(1, n))
        specs.append(pl.BlockSpec((1, tn), lambda i, j: (0, j)))
    body = functools.partial(_mm_body, nd=len(dots), nt=len(tiles), nr=len(rows), epi=epi)
    return pl.pallas_call(
        body,
        grid=(m // tm, n // tn),
        in_specs=specs,
        out_specs=pl.BlockSpec((tm, tn), lambda i, j: (i, j)),
        out_shape=jax.ShapeDtypeStruct((m, n), out_dtype),
        compiler_params=_params(("parallel", "parallel")),
        name=name,
    )(*ins)


def _hgrn_consts(c):
    idx = np.arange(c)
    mats = [(idx[:, None] >= idx[None, :])]
    masks = [np.eye(c, dtype=bool)]
    lq, lk = [], []
    b = 1
    while b < c:
        blk = idx // (2 * b)
        second = (idx % (2 * b)) >= b
        bnd = blk * 2 * b + b - 1
        lq.append((idx[None, :] > bnd[:, None]) & (idx[None, :] <= idx[:, None]) & second[:, None])
        lk.append((idx[None, :] > idx[:, None]) & (idx[None, :] <= bnd[:, None]) & (~second)[:, None])
        masks.append((blk[:, None] == blk[None, :]) & second[:, None] & (~second)[None, :])
        b *= 2
    lmat = np.concatenate(mats + lq + lk, axis=0).astype(np.float32)
    return jnp.asarray(lmat), jnp.asarray(np.stack(masks).astype(np.float32)), len(lq)


def _hgrn_body(q_ref, f_ref, i_ref, g_ref, lb_ref, gn_ref, s0_ref, l_ref, mask_ref, o_ref, so_ref, st_ref, *, c, nl, nchunks):
    tb = pl.program_id(2)

    @pl.when(tb == 0)
    def _():
        st_ref[...] = s0_ref[0, 0].T

    lb = lb_ref[...]
    gn = gn_ref[...]

    def chunk(ci, carry):
        r0 = pl.multiple_of(ci * c, c)
        rows = pl.ds(r0, c)
        q = q_ref[rows, :]
        fg = lb + (1.0 - lb) * jax.nn.sigmoid(f_ref[rows, :])
        gl = jnp.log(fg)
        kk = 1.0 - fg
        qs = q * jax.nn.sigmoid(q)
        v = i_ref[rows, :]
        d = jnp.dot(l_ref[...], gl, preferred_element_type=f32, precision=HI)
        gcum = d[0:c]
        a = mask_ref[0] * _nt(qs, kk)
        for l in range(nl):
            dq = d[(1 + l) * c:(2 + l) * c]
            dk = d[(1 + nl + l) * c:(2 + nl + l) * c]
            a = a + mask_ref[1 + l] * _nt(qs * jnp.exp(dq), kk * jnp.exp(dk))
        st = st_ref[...]
        o = _nn(a, v) + _nt(qs * jnp.exp(gcum), st)
        gend = gcum[c - 1:c, :]
        st_ref[...] = st * jnp.exp(gend) + _tn(v, kk * jnp.exp(gend - gcum))
        ms = jnp.mean(o * o, axis=-1, keepdims=True)
        on = o * lax.rsqrt(ms + RMS_EPS) * gn
        o_ref[rows, :] = (on * jax.nn.sigmoid(g_ref[rows, :])).astype(o_ref.dtype)
        return carry

    lax.fori_loop(0, nchunks, chunk, 0)

    @pl.when(tb == pl.num_programs(2) - 1)
    def _():
        so_ref[0, 0] = st_ref[...].T


def _hgrn(z, lb, gnorm, s0, row_off, nseq, t):
    d = z.shape[1] // 4
    heads = d // HG_HEAD
    c = min(CHUNK, t)
    tblk = _pick(t, (512, 256, 128, 64, 32))
    nblk = t // tblk
    off = row_off // tblk
    lmat, masks, nl = _hgrn_consts(c)

    def zspec(sec):
        return pl.BlockSpec((tblk, HG_HEAD), lambda s, h, b: (off + s * nblk + b, sec * heads + h))

    body = functools.partial(_hgrn_body, c=c, nl=nl, nchunks=tblk // c)
    return pl.pallas_call(
        body,
        grid=(nseq, heads, nblk),
        in_specs=[zspec(0), zspec(1), zspec(2), zspec(3),
                  pl.BlockSpec((1, HG_HEAD), lambda s, h, b: (0, h)),
                  pl.BlockSpec((1, HG_HEAD), lambda s, h, b: (0, 0)),
                  pl.BlockSpec((1, 1, HG_HEAD, HG_HEAD), lambda s, h, b: (s, h, 0, 0)),
                  pl.BlockSpec(lmat.shape, lambda s, h, b: (0, 0)),
                  pl.BlockSpec(masks.shape, lambda s, h, b: (0, 0, 0))],
        out_specs=[pl.BlockSpec((tblk, HG_HEAD), lambda s, h, b: (s * nblk + b, h)),
                   pl.BlockSpec((1, 1, HG_HEAD, HG_HEAD), lambda s, h, b: (s, h, 0, 0))],
        out_shape=[jax.ShapeDtypeStruct((nseq * t, d), bf16),
                   jax.ShapeDtypeStruct((nseq, heads, HG_HEAD, HG_HEAD), f32)],
        scratch_shapes=[pltpu.VMEM((HG_HEAD, HG_HEAD), f32)],
        compiler_params=_params(("parallel", "parallel", "arbitrary")),
        name="hgrn2_scan",
    )(z, z, z, z, lb.reshape(1, d), gnorm.reshape(1, HG_HEAD), s0, lmat, masks)


def _shift_body(x_ref, halo_ref, sh_ref, mu_ref, *o_refs):
    tb = pl.program_id(1)
    x = x_ref[...]
    first = jnp.where(tb == 0, sh_ref[0], halo_ref[7:8, :])
    row = lax.broadcasted_iota(jnp.int32, x.shape, 0)
    xx = jnp.where(row == 0, first, pltpu.roll(x, 1, axis=0)) - x
    for j, o_ref in enumerate(o_refs):
        o_ref[...] = (x + xx * mu_ref[j:j + 1, :]).astype(o_ref.dtype)


def _shift_mix(xn, shift0, mu, row_off, nseq, t):
    d = xn.shape[1]
    tblk = _pick(t, (512, 256, 128, 64, 32))
    td = _pick(d, (1024, 512, 256, 128))
    nblk = t // tblk
    off = row_off // tblk
    hb = tblk // 8
    return pl.pallas_call(
        _shift_body,
        grid=(nseq, nblk, d // td),
        in_specs=[pl.BlockSpec((tblk, td), lambda s, b, j: (off + s * nblk + b, j)),
                  pl.BlockSpec((8, td), lambda s, b, j: (jnp.maximum((off + s * nblk + b) * hb - 1, 0), j)),
                  pl.BlockSpec((1, 1, td), lambda s, b, j: (s, 0, j)),
                  pl.BlockSpec((6, td), lambda s, b, j: (0, j))],
        out_specs=[pl.BlockSpec((tblk, td), lambda s, b, j: (s * nblk + b, j))] * 6,
        out_shape=[jax.ShapeDtypeStruct((nseq * t, d), bf16)] * 6,
        compiler_params=_params(("parallel", "parallel", "parallel")),
        name="rwkv_shift_mix",
    )(xn, xn, shift0.reshape(nseq, 1, d), mu)


def _rwkv_body(r_ref, k_ref, v_ref, w_ref, a_ref, g_ref, kk_ref, ka_ref, rk_ref, lnw_ref, lbias_ref, s0_ref,
               o_ref, so_ref, st_ref, *, c, nchunks):
    tb = pl.program_id(2)
    n = RW_HEAD

    @pl.when(tb == 0)
    def _():
        st_ref[...] = s0_ref[0]

    ri = lax.broadcasted_iota(jnp.int32, (c, c), 0)
    ci_ = lax.broadcasted_iota(jnp.int32, (c, c), 1)
    low = (ri >= ci_).astype(f32)
    slow = (ri > ci_).astype(f32)
    blkd = (ri // RW_SUB == ci_ // RW_SUB).astype(f32)
    eye = (ri == ci_).astype(f32)
    nb = c // RW_SUB

    def chunk(ci, carry):
        r0 = pl.multiple_of(ci * c, c)
        rows = pl.ds(r0, c)
        r = r_ref[rows, :]
        k = k_ref[rows, :]
        v = v_ref[rows, :]
        g = g_ref[rows, :]
        lw = -jnp.exp(-jax.nn.softplus(-w_ref[rows, :]) - 0.5)
        a = jax.nn.sigmoid(a_ref[rows, :])
        kkp = k * kk_ref[...]
        k2 = k * (1.0 + (a - 1.0) * ka_ref[...])
        cum = jnp.dot(low, lw, preferred_element_type=f32, precision=HI)
        cend = cum[c - 1:c, :]
        e_excl = jnp.exp(cum - lw)
        e_incl = jnp.exp(cum)
        e_inv = jnp.exp(-cum)
        e_end = jnp.exp(cend - cum)
        gam = jnp.exp(cend)
        rt = r * e_incl
        kh = k2 * e_inv
        kb = k2 * e_end
        rkk = r * k2 * rk_ref[...]
        for e in range(2):
            sl = slice(e * n, (e + 1) * n)
            kk = kkp[:, sl]
            kk = kk / jnp.maximum(jnp.sqrt(jnp.sum(kk * kk, axis=-1, keepdims=True)), 1e-12)
            b = kk * a[:, sl]
            kkt = kk * e_excl[:, sl]
            bh = b * e_inv[:, sl]
            bb = b * e_end[:, sl]
            ve = v[:, sl]
            mk = slow * _nt(kkt, kh[:, sl])
            mb = slow * _nt(kkt, bh)
            pk = low * _nt(rt[:, sl], kh[:, sl])
            pb = low * _nt(rt[:, sl], bh)
            md = mb * blkd
            m2 = _nn3(md, md)
            m4 = _nn3(m2, m2)
            m8 = _nn3(m4, m4)
            td = _nn3(_nn3(_nn3(eye - md, eye + m2), eye + m4), eye + m8)
            if nb > 1:
                nn_ = _nn3(td, mb - md)
                tn_ = eye - nn_
                p = nn_
                step = 2
                while step < nb:
                    p = _nn3(p, p)
                    tn_ = _nn3(tn_, eye + p)
                    step *= 2
                tm = _nn3(tn_, td)
            else:
                tm = td
            s = st_ref[e]
            u = _nn3(tm, _nt(kkt, s) + _nn(mk, ve))
            y = _nt(rt[:, sl], s) + _nn(pk, ve) - _nn(pb, u)
            st_ref[e] = s * gam[:, sl] + _tn(ve, kb[:, sl]) - _tn(u, bb)
            mean = jnp.mean(y, axis=-1, keepdims=True)
            yc = y - mean
            var = jnp.mean(yc * yc, axis=-1, keepdims=True)
            yn = yc * lax.rsqrt(var + GN_EPS) * lnw_ref[:, sl] + lbias_ref[:, sl]
            bonus = jnp.sum(rkk[:, sl], axis=-1, keepdims=True) * ve
            o_ref[rows, sl] = ((yn + bonus) * g[:, sl]).astype(o_ref.dtype)
        return carry

    lax.fori_loop(0, nchunks, chunk, 0)

    @pl.when(tb == pl.num_programs(2) - 1)
    def _():
        so_ref[0] = st_ref[...]


def _rwkv(acts, vecs, s0, row_off, nseq, t):
    d = acts[0].shape[1]
    heads = d // RW_HEAD
    c = min(CHUNK, t)
    tblk = _pick(t, (256, 128, 64, 32))
    nblk = t // tblk
    off = row_off // tblk
    wp = 2 * RW_HEAD
    aspec = pl.BlockSpec((tblk, wp), lambda s, h, b: (off + s * nblk + b, h))
    vspec = pl.BlockSpec((1, wp), lambda s, h, b: (0, h))
    sspec = pl.BlockSpec((1, 2, RW_HEAD, RW_HEAD), lambda s, h, b: (s, h, 0, 0))
    body = functools.partial(_rwkv_body, c=c, nchunks=tblk // c)
    return pl.pallas_call(
        body,
        grid=(nseq, heads // 2, nblk),
        in_specs=[aspec] * 6 + [vspec] * 5 + [sspec],
        out_specs=[pl.BlockSpec((tblk, wp), lambda s, h, b: (s * nblk + b, h)), sspec],
        out_shape=[jax.ShapeDtypeStruct((nseq * t, d), bf16),
                   jax.ShapeDtypeStruct((nseq, heads, RW_HEAD, RW_HEAD), f32)],
        scratch_shapes=[pltpu.VMEM((2, RW_HEAD, RW_HEAD), f32)],
        compiler_params=_params(("parallel", "parallel", "arbitrary")),
        name="rwkv7_scan",
    )(*acts, *[x.reshape(1, d) for x in vecs], s0)


def _conv_body(hg_ref, halo_ref, hu_ref, c0_ref, cw_ref, cb_ref, o_ref, cn_ref):
    tb = pl.program_id(1)
    x = hg_ref[...].astype(f32)
    rows = x.shape[0]
    c0 = c0_ref[0]
    halo = halo_ref[...].astype(f32)
    hm1 = jnp.where(tb == 0, c0[1:2, :], halo[7:8, :])
    hm2 = jnp.where(tb == 0, c0[0:1, :], halo[6:7, :])
    row = lax.broadcasted_iota(jnp.int32, x.shape, 0)
    p1 = jnp.where(row == 0, hm1, pltpu.roll(x, 1, axis=0))
    p2 = jnp.where(row == 0, hm2, jnp.where(row == 1, hm1, pltpu.roll(x, 2, axis=0)))
    cw = cw_ref[...]
    hc = cb_ref[...] + (cw[0:1, :] * p2 + cw[1:2, :] * p1 + cw[2:3, :] * x)
    act = 0.5 * hc * (1.0 + lax.erf(hc * 0.7071067811865476))
    o_ref[...] = (act * hu_ref[...].astype(f32)).astype(o_ref.dtype)

    @pl.when(tb == pl.num_programs(1) - 1)
    def _():
        cn_ref[0] = x[rows - 2:rows, :]


def _conv_gate(up, c0, cw, cb, row_off, nseq, t):
    f = up.shape[1] // 2
    tblk = _pick(t, (512, 256, 128, 64, 32))
    tf = _pick(f, (512, 256, 128))
    nblk = t // tblk
    nf = f // tf
    off = row_off // tblk
    hb = tblk // 8
    return pl.pallas_call(
        _conv_body,
        grid=(nseq, nblk, nf),
        in_specs=[pl.BlockSpec((tblk, tf), lambda s, b, j: (off + s * nblk + b, j)),
                  pl.BlockSpec((8, tf), lambda s, b, j: (jnp.maximum((off + s * nblk + b) * hb - 1, 0), j)),
                  pl.BlockSpec((tblk, tf), lambda s, b, j: (off + s * nblk + b, nf + j)),
                  pl.BlockSpec((1, 2, tf), lambda s, b, j: (s, 0, j)),
                  pl.BlockSpec((3, tf), lambda s, b, j: (0, j)),
                  pl.BlockSpec((1, tf), lambda s, b, j: (0, j))],
        out_specs=[pl.BlockSpec((tblk, tf), lambda s, b, j: (s * nblk + b, j)),
                   pl.BlockSpec((1, 2, tf), lambda s, b, j: (s, 0, j))],
        out_shape=[jax.ShapeDtypeStruct((nseq * t, f), bf16),
                   jax.ShapeDtypeStruct((nseq, 2, f), f32)],
        compiler_params=_params(("parallel", "arbitrary", "arbitrary")),
        name="convffn_gate",
    )(up, up, up, c0, cw, cb.reshape(1, f))


def _first(accs, tiles, rows):
    return accs[0]


def _resid(accs, tiles, rows):
    return tiles[0] + accs[0]


def _bias(accs, tiles, rows):
    return accs[0] + rows[0]


def _ple(accs, tiles, rows):
    return tiles[0] + jax.nn.sigmoid(accs[0]) * accs[1]


def _tanh(accs, tiles, rows):
    return jnp.tanh(accs[0])


def _sigm(accs, tiles, rows):
    return jax.nn.sigmoid(accs[0])


def kernel(x_prompt, x_sample, p_prompt, p_sample, state_hgrn, state_rwkv, state_shift, state_ffn_conv, norm_mix, norm_ffn, norm_ple, norm_final, hg_w_in, hg_lb_logits, hg_gnorm, hg_w_o, rw_mu, rw_w_rkv, rw_w0, rw_w1, rw_w2, rw_a0, rw_a1, rw_a2, rw_g1, rw_g2, rw_k_k, rw_k_a, rw_r_k, rw_lnx_w, rw_lnx_b, rw_w_o, ffn_w_up, ffn_conv_w, ffn_conv_b, ffn_w_down, ple_w_proj, ple_w_gate):
    bp, tp, d = x_prompt.shape
    bs, ts, _ = x_sample.shape
    depth = norm_mix.shape[0]
    mp, ms = bp * tp, bs * ts
    groups = ((0, bp, tp), (mp, bs, ts))
    f_ff = ffn_conv_b.shape[1]
    cast = lambda w: w.astype(bf16)

    h = jnp.concatenate([x_prompt.reshape(mp, d), x_sample.reshape(ms, d)], axis=0)
    p_all = jnp.concatenate([p_prompt.reshape(depth, mp, -1), p_sample.reshape(depth, ms, -1)], axis=1).astype(bf16)
    lb_all = jnp.cumsum(jax.nn.softmax(hg_lb_logits.astype(f32), axis=0), axis=0)

    hg_states = [[], []]
    rw_states = [[], []]
    sh_states = [[], []]
    cv_states = [[], []]
    for i in range(depth):
        j = i // 2
        if i % 2 == 0:
            xn = _rmsnorm(h, norm_mix[i], bf16)
            z = _mm([(xn, cast(hg_w_in[j]))], _first, f32, name="hgrn_in")
            outs = []
            for gi, (off, nseq, t) in enumerate(groups):
                s0 = jnp.zeros((nseq, d // HG_HEAD, HG_HEAD, HG_HEAD), f32) if gi == 0 else state_hgrn[j].astype(f32)
                o, s_new = _hgrn(z, lb_all[j], hg_gnorm[j], s0, off, nseq, t)
                outs.append(o)
                hg_states[gi].append(s_new)
            mix_in = jnp.concatenate(outs, axis=0)
            h = _mm([(mix_in, cast(hg_w_o[j]))], _resid, f32, tiles=(h,), name="hgrn_out")
        else:
            xn = _rmsnorm(h, norm_mix[i], f32)
            mixes = []
            for gi, (off, nseq, t) in enumerate(groups):
                sh0 = jnp.zeros((nseq, d), f32) if gi == 0 else state_shift[j]
                mixes.append(_shift_mix(xn, sh0, rw_mu[j], off, nseq, t))
                sh_states[gi].append(xn[off:off + nseq * t].reshape(nseq, t, d)[:, -1])
            xr, xw, xk, xv, xa, xg = (jnp.concatenate([mixes[0][q], mixes[1][q]], axis=0) for q in range(6))
            wr, wk, wv = (cast(rw_w_rkv[j][:, q * d:(q + 1) * d]) for q in range(3))
            r = _mm([(xr, wr)], _first, f32, name="rwkv_r")
            k = _mm([(xk, wk)], _first, f32, name="rwkv_k")
            v = _mm([(xv, wv)], _first, f32, name="rwkv_v")
            w_lin = _mm([(_mm([(xw, cast(rw_w1[j]))], _tanh, bf16, name="rwkv_w1"), cast(rw_w2[j]))],
                        _bias, f32, rows=(rw_w0[j],), name="rwkv_w2")
            a_lin = _mm([(_mm([(xa, cast(rw_a1[j]))], _first, bf16, name="rwkv_a1"), cast(rw_a2[j]))],
                        _bias, f32, rows=(rw_a0[j],), name="rwkv_a2")
            g = _mm([(_mm([(xg, cast(rw_g1[j]))], _sigm, bf16, name="rwkv_g1"), cast(rw_g2[j]))],
                    _first, f32, name="rwkv_g2")
            vecs = (rw_k_k[j], rw_k_a[j], rw_r_k[j].reshape(d), rw_lnx_w[j], rw_lnx_b[j])
            outs = []
            for gi, (off, nseq, t) in enumerate(groups):
                s0 = jnp.zeros((nseq, d // RW_HEAD, RW_HEAD, RW_HEAD), f32) if gi == 0 else state_rwkv[j].astype(f32)
                o, s_new = _rwkv((r, k, v, w_lin, a_lin, g), vecs, s0, off, nseq, t)
                outs.append(o)
                rw_states[gi].append(s_new)
            mix_in = jnp.concatenate(outs, axis=0)
            h = _mm([(mix_in, cast(rw_w_o[j]))], _resid, f32, tiles=(h,), name="rwkv_out")

        xn = _rmsnorm(h, norm_ffn[i], bf16)
        up = _mm([(xn, cast(ffn_w_up[i]))], _first, bf16, name="ffn_up")
        acts = []
        for gi, (off, nseq, t) in enumerate(groups):
            c0 = jnp.zeros((nseq, 2, f_ff), f32) if gi == 0 else state_ffn_conv[i]
            act, c_new = _conv_gate(up, c0, ffn_conv_w[i], ffn_conv_b[i], off, nseq, t)
            acts.append(act)
            cv_states[gi].append(c_new)
        act = jnp.concatenate(acts, axis=0)
        h = _mm([(act, cast(ffn_w_down[i]))], _resid, f32, tiles=(h,), tm=512, tn=256, name="ffn_down")

        xn = _rmsnorm(h, norm_ple[i], bf16)
        h = _mm([(xn, cast(ple_w_gate[i])), (p_all[i], cast(ple_w_proj[i]))], _ple, f32, tiles=(h,), name="ple")

    y = _rmsnorm(h, norm_final, f32)
    y_prompt = y[:mp].reshape(bp, tp, d)
    y_sample = y[mp:].reshape(bs, ts, d)
    st = lambda xs: jnp.stack(xs)
    return (y_prompt, y_sample,
            st(hg_states[0]), st(rw_states[0]), st(sh_states[0]), st(cv_states[0]),
            st(hg_states[1]), st(rw_states[1]), st(sh_states[1]), st(cv_states[1]))
```

```python
import functools

import numpy as np
import jax
import jax.numpy as jnp
from jax import lax
from jax.experimental import pallas as pl
from jax.experimental.pallas import tpu as pltpu

f32 = jnp.float32
bf16 = jnp.bfloat16

RMS_EPS = 1e-6
GN_EPS = 64e-5
HG_HEAD = 128
HG_CHUNK = 128
HG_HEADS_PER_STEP = 4
RW_HEAD = 64
RW_CHUNK = 64
RW_PAIRS_PER_STEP = 8
VMEM_LIMIT = 50 * 1024 * 1024


def _pick(n, prefs):
    for p in prefs:
        if n % p == 0:
            return p
    return n


def _params(sem):
    return pltpu.CompilerParams(dimension_semantics=sem, vmem_limit_bytes=VMEM_LIMIT)


def _nn(a, b):
    return jnp.dot(a.astype(bf16), b.astype(bf16), preferred_element_type=f32)


def _nt(a, b):
    return lax.dot_general(a.astype(bf16), b.astype(bf16), (((1,), (1,)), ((), ())), preferred_element_type=f32)


def _tn(a, b):
    return lax.dot_general(a.astype(bf16), b.astype(bf16), (((0,), (0,)), ((), ())), preferred_element_type=f32)


def _split3(x):
    hi = x.astype(bf16)
    r1 = x - hi.astype(f32)
    mid = r1.astype(bf16)
    lo = (r1 - mid.astype(f32)).astype(bf16)
    return jnp.concatenate([hi, mid, lo], axis=1)


def _sum3(d3, w):
    return d3[:, 0:w] + (d3[:, w:2 * w] + d3[:, 2 * w:3 * w])


def _round_robin(gens):
    for _ in zip(*gens):
        pass


def _group_call(body, grid, in_specs, ins, out_specs, out_shapes, dst, scratch, sem, name):
    in_specs = list(in_specs)
    ins = list(ins)
    aliases = {}
    if dst is not None:
        for q, dq in enumerate(dst):
            aliases[len(ins)] = q
            in_specs.append(pl.BlockSpec(memory_space=pl.ANY))
            ins.append(dq)
    return pl.pallas_call(
        functools.partial(body, ndst=0 if dst is None else len(dst)),
        grid=grid, in_specs=in_specs, out_specs=out_specs, out_shape=out_shapes,
        scratch_shapes=scratch, input_output_aliases=aliases,
        compiler_params=_params(sem), name=name,
    )(*ins)


def _rms_body(x_ref, g_ref, o_ref):
    x = x_ref[...]
    ms = jnp.mean(x * x, axis=-1, keepdims=True)
    o_ref[...] = (x * lax.rsqrt(ms + RMS_EPS) * g_ref[...]).astype(o_ref.dtype)


def _rmsnorm(x, g, out_dtype):
    m, d = x.shape
    tm = _pick(m, (256, 128, 64, 32, 16, 8))
    return pl.pallas_call(
        _rms_body,
        grid=(m // tm,),
        in_specs=[pl.BlockSpec((tm, d), lambda i: (i, 0)), pl.BlockSpec((1, d), lambda i: (0, 0))],
        out_specs=pl.BlockSpec((tm, d), lambda i: (i, 0)),
        out_shape=jax.ShapeDtypeStruct((m, d), out_dtype),
        compiler_params=_params(("parallel",)),
        name="rmsnorm",
    )(x, g.reshape(1, d))


def _mm_body(*refs, nd, nt, nr, epi):
    dots = refs[: 2 * nd]
    tiles = refs[2 * nd: 2 * nd + nt]
    rows = refs[2 * nd + nt: 2 * nd + nt + nr]
    o_ref = refs[-1]
    accs = [jnp.dot(dots[2 * i][...], dots[2 * i + 1][...], preferred_element_type=f32) for i in range(nd)]
    o_ref[...] = epi(accs, [t[...] for t in tiles], [r[...] for r in rows]).astype(o_ref.dtype)


def _mm(dots, epi, out_dtype, tiles=(), rows=(), tm=1024, tn=512, name="mm"):
    m = dots[0][0].shape[0]
    n = dots[0][1].shape[1]
    tm = _pick(m, (tm, 512, 256, 128, 64, 32, 16, 8))
    tn = _pick(n, (tn, 256, 128))
    ins, specs = [], []
    for x, w in dots:
        k = x.shape[1]
        ins += [x, w]
        specs += [pl.BlockSpec((tm, k), lambda i, j: (i, 0)), pl.BlockSpec((k, tn), lambda i, j: (0, j))]
    for t in tiles:
        ins.append(t)
        specs.append(pl.BlockSpec((tm, tn), lambda i, j: (i, j)))
    for r in rows:
        ins.append(r.reshape(1, n))
        specs.append(pl.BlockSpec((1, tn), lambda i, j: (0, j)))
    body = functools.partial(_mm_body, nd=len(dots), nt=len(tiles), nr=len(rows), epi=epi)
    return pl.pallas_call(
        body,
        grid=(m // tm, n // tn),
        in_specs=specs,
        out_specs=pl.BlockSpec((tm, tn), lambda i, j: (i, j)),
        out_shape=jax.ShapeDtypeStruct((m, n), out_dtype),
        compiler_params=_params(("parallel", "parallel")),
        name=name,
    )(*ins)


def _hgrn_consts(c):
    idx = np.arange(c)
    mats = [(idx[:, None] >= idx[None, :])]
    masks = [np.eye(c, dtype=bool)]
    b = 1
    while b < c:
        blk = idx // (2 * b)
        second = (idx % (2 * b)) >= b
        bnd = blk * 2 * b + b - 1
        lq = (idx[None, :] > bnd[:, None]) & (idx[None, :] <= idx[:, None]) & second[:, None]
        lk = (idx[None, :] > idx[:, None]) & (idx[None, :] <= bnd[:, None]) & (~second)[:, None]
        mats.append(lq | lk)
        masks.append((blk[:, None] == blk[None, :]) & second[:, None] & (~second)[None, :])
        b *= 2
    lmat = np.concatenate(mats, axis=0).astype(np.float32)
    return jnp.asarray(lmat, dtype=bf16), jnp.asarray(np.stack(masks).astype(np.float32)), len(mats) - 1


def _hgrn_body(q_ref, f_ref, i_ref, g_ref, lb_ref, gn_ref, s0_ref, l_ref, mask_ref, *rest, c, nl, nchunks, nh, ndst):
    o_ref, so_ref, st_ref = rest[ndst:]
    tb = pl.program_id(2)
    w = HG_HEAD

    @pl.when(tb == 0)
    def _():
        for hh in range(nh):
            st_ref[hh] = s0_ref[0, hh].T

    gn = gn_ref[...]

    def chunk(ci, carry):
        r0 = pl.multiple_of(ci * c, c)
        rows = pl.ds(r0, c)
        def head_steps(hh):
            cols = slice(hh * w, (hh + 1) * w)
            lb = lb_ref[:, cols]
            q = q_ref[rows, cols]
            fg = lb + (1.0 - lb) * jax.nn.sigmoid(f_ref[rows, cols])
            gl = jnp.log(fg)
            kk = 1.0 - fg
            qs = q * jax.nn.sigmoid(q)
            v = i_ref[rows, cols]
            d = _sum3(jnp.dot(l_ref[...], _split3(gl), preferred_element_type=f32), w)
            a = mask_ref[0] * _nt(qs, kk)
            yield
            gcum = d[0:c]
            gend = gcum[c - 1:c, :]
            st = st_ref[hh]
            oi = _nt(qs * jnp.exp(gcum), st)
            st_ref[hh] = st * jnp.exp(gend) + _tn(v, kk * jnp.exp(gend - gcum))
            for l in range(nl):
                e = jnp.exp(d[(1 + l) * c:(2 + l) * c])
                a = a + mask_ref[1 + l] * _nt(qs * e, kk * e)
            yield
            o = _nn(a, v) + oi
            yield
            ms = jnp.mean(o * o, axis=-1, keepdims=True)
            on = o * lax.rsqrt(ms + RMS_EPS) * gn
            o_ref[rows, cols] = (on * jax.nn.sigmoid(g_ref[rows, cols])).astype(o_ref.dtype)
            yield

        _round_robin([head_steps(hh) for hh in range(nh)])
        return carry

    lax.fori_loop(0, nchunks, chunk, 0)

    @pl.when(tb == pl.num_programs(2) - 1)
    def _():
        for hh in range(nh):
            so_ref[0, hh] = st_ref[hh].T


def _hgrn(z, lb, gnorm, s0, row_off, nseq, t, dst):
    m = z.shape[0]
    d = z.shape[1] // 4
    heads = d // HG_HEAD
    c = min(HG_CHUNK, t)
    nh = _pick(heads, (HG_HEADS_PER_STEP, 2, 1))
    tblk = _pick(t, (512, 256, 128, 64, 32))
    nblk = t // tblk
    off = row_off // tblk
    hgrp = heads // nh
    wblk = nh * HG_HEAD
    lmat, masks, nl = _hgrn_consts(c)

    def zspec(sec):
        return pl.BlockSpec((tblk, wblk), lambda s, h, b: (off + s * nblk + b, sec * hgrp + h))

    sspec = pl.BlockSpec((1, nh, HG_HEAD, HG_HEAD), lambda s, h, b: (s, h, 0, 0))
    body = functools.partial(_hgrn_body, c=c, nl=nl, nchunks=tblk // c, nh=nh)
    return _group_call(
        body, (nseq, hgrp, nblk),
        [zspec(0), zspec(1), zspec(2), zspec(3),
         pl.BlockSpec((1, wblk), lambda s, h, b: (0, h)),
         pl.BlockSpec((1, HG_HEAD), lambda s, h, b: (0, 0)),
         sspec,
         pl.BlockSpec(lmat.shape, lambda s, h, b: (0, 0)),
         pl.BlockSpec(masks.shape, lambda s, h, b: (0, 0, 0))],
        [z, z, z, z, lb.reshape(1, d), gnorm.reshape(1, HG_HEAD), s0, lmat, masks],
        [pl.BlockSpec((tblk, wblk), lambda s, h, b: (off + s * nblk + b, h)), sspec],
        [jax.ShapeDtypeStruct((m, d), bf16), jax.ShapeDtypeStruct((nseq, heads, HG_HEAD, HG_HEAD), f32)],
        dst, [pltpu.VMEM((nh, HG_HEAD, HG_HEAD), f32)], ("parallel", "parallel", "arbitrary"), "hgrn2_scan")


def _shift_body(x_ref, halo_ref, sh_ref, mu_ref, *rest, ndst):
    o_refs = rest[ndst:]
    tb = pl.program_id(2)
    x = x_ref[...]
    first = jnp.where(tb == 0, sh_ref[0], halo_ref[7:8, :])
    row = lax.broadcasted_iota(jnp.int32, x.shape, 0)
    xx = jnp.where(row == 0, first, pltpu.roll(x, 1, axis=0)) - x
    for j, o_ref in enumerate(o_refs):
        o_ref[...] = (x + xx * mu_ref[j:j + 1, :]).astype(o_ref.dtype)


def _shift_mix(xn, shift0, mu, row_off, nseq, t, dst):
    m, d = xn.shape
    tblk = _pick(t, (512, 256, 128, 64, 32))
    td = _pick(d, (1024, 512, 256, 128))
    nblk = t // tblk
    off = row_off // tblk
    hb = tblk // 8
    return _group_call(
        _shift_body, (nseq, d // td, nblk),
        [pl.BlockSpec((tblk, td), lambda s, j, b: (off + s * nblk + b, j)),
         pl.BlockSpec((8, td), lambda s, j, b: (jnp.maximum((off + s * nblk + b) * hb - 1, 0), j)),
         pl.BlockSpec((1, 1, td), lambda s, j, b: (s, 0, j)),
         pl.BlockSpec((6, td), lambda s, j, b: (0, j))],
        [xn, xn, shift0.reshape(nseq, 1, d), mu],
        [pl.BlockSpec((tblk, td), lambda s, j, b: (off + s * nblk + b, j))] * 6,
        [jax.ShapeDtypeStruct((m, d), bf16)] * 6,
        dst, [], ("parallel", "parallel", "arbitrary"), "rwkv_shift_mix")


def _rwkv_body(r_ref, k_ref, v_ref, w_ref, a_ref, g_ref, kk_ref, ka_ref, rk_ref, lnw_ref, lnb_ref, s0_ref, *rest,
               c, nchunks, npair, ndst):
    o_ref, so_ref, st_ref = rest[ndst:]
    tb = pl.program_id(2)
    n = RW_HEAD
    w = 2 * n
    c2 = 2 * c
    i32 = jnp.int32

    @pl.when(tb == 0)
    def _():
        st_ref[...] = jnp.zeros(st_ref.shape, f32)
        for p in range(npair):
            st_ref[p, 0:n, 0:n] = s0_ref[0, 2 * p]
            st_ref[p, n:w, n:w] = s0_ref[0, 2 * p + 1]

    ri = lax.broadcasted_iota(i32, (c2, c2), 0)
    cj = lax.broadcasted_iota(i32, (c2, c2), 1)
    same = (ri // c) == (cj // c)
    low_bd = (same & (ri >= cj)).astype(f32)
    slow_bd = (same & (ri > cj)).astype(f32)
    eye = (ri == cj).astype(f32)
    lvl = []
    b = 1
    while b < c:
        lvl.append(((ri // (2 * b) == cj // (2 * b)) & (ri % (2 * b) >= b) & (cj % (2 * b) < b)).astype(f32))
        b *= 2
    lowc = (lax.broadcasted_iota(i32, (c, c), 0) >= lax.broadcasted_iota(i32, (c, c), 1)).astype(bf16)
    m0 = lax.broadcasted_iota(i32, (c, w), 1) < n
    hmask = (lax.broadcasted_iota(i32, (c2, w), 0) // c) == (lax.broadcasted_iota(i32, (c2, w), 1) // n)

    def stack(x):
        return jnp.concatenate([jnp.where(m0, x, 0.0), jnp.where(m0, 0.0, x)], axis=0)

    def halfsums(x):
        s0_ = jnp.sum(jnp.where(m0, x, 0.0), axis=-1, keepdims=True)
        s1_ = jnp.sum(jnp.where(m0, 0.0, x), axis=-1, keepdims=True)
        return s0_, s1_

    def chunk(ci, carry):
        r0 = pl.multiple_of(ci * c, c)
        rows = pl.ds(r0, c)
        def pair_steps(p):
            cols = slice(p * w, (p + 1) * w)
            r = r_ref[rows, cols]
            k = k_ref[rows, cols]
            v = v_ref[rows, cols]
            lw = -jnp.exp(-jax.nn.softplus(-w_ref[rows, cols]) - 0.5)
            a = jax.nn.sigmoid(a_ref[rows, cols])
            kkp = k * kk_ref[:, cols]
            k2 = k * (1.0 + (a - 1.0) * ka_ref[:, cols])
            cum = _sum3(jnp.dot(lowc, _split3(lw), preferred_element_type=f32), w)
            yield
            cend = cum[c - 1:c, :]
            e_incl = jnp.exp(cum)
            e_inv = jnp.exp(-cum)
            e_end = jnp.exp(cend - cum)
            gam = jnp.exp(cend)
            n0, n1 = halfsums(kkp * kkp)
            rn = 1.0 / jnp.maximum(jnp.sqrt(jnp.concatenate([n0, n1], axis=0)), 1e-12)
            ka_ = kkp * a
            kkt_s = stack(kkp * jnp.exp(cum - lw)) * rn
            bh_s = stack(ka_ * e_inv) * rn
            bb_s = stack(ka_ * e_end) * rn
            rt_s = stack(r * e_incl)
            kh_s = stack(k2 * e_inv)
            kb_s = stack(k2 * e_end)
            v_s = stack(v)
            gram = _nt(jnp.concatenate([kkt_s, rt_s], axis=0), jnp.concatenate([kh_s, bh_s], axis=0))
            yield
            mk = slow_bd * gram[0:c2, 0:c2]
            mb = slow_bd * gram[0:c2, c2:2 * c2]
            pk = low_bd * gram[c2:2 * c2, 0:c2]
            pb = low_bd * gram[c2:2 * c2, c2:2 * c2]
            mkv = _nn(mk, v_s)
            pkv = _nn(pk, v_s)
            vk = _tn(v_s, kb_s)
            tm = eye - lvl[0] * mb
            for l in range(1, len(lvl)):
                tc = _nn(tm, lvl[l] * mb)
                yield
                tm = tm - _nn(tc, tm)
                yield
            wt = _nn(tm, kkt_s)
            u0 = _nn(tm, mkv)
            yield
            s = st_ref[p]
            u = _nt(wt, s) + u0
            yr = _nt(rt_s, s) + pkv
            yield
            y_s = yr - _nn(pb, u)
            st_ref[p] = s * gam + vk - _tn(u, bb_s)
            yield
            mean = jnp.sum(y_s, axis=-1, keepdims=True) * (1.0 / n)
            yc = jnp.where(hmask, y_s - mean, 0.0)
            var = jnp.sum(yc * yc, axis=-1, keepdims=True) * (1.0 / n)
            yn_s = yc * lax.rsqrt(var + GN_EPS)
            yn = (yn_s[0:c] + yn_s[c:c2]) * lnw_ref[:, cols] + lnb_ref[:, cols]
            b0, b1 = halfsums(r * k2 * rk_ref[:, cols])
            bonus = jnp.where(m0, b0, b1) * v
            o_ref[rows, cols] = ((yn + bonus) * g_ref[rows, cols]).astype(o_ref.dtype)
            yield

        _round_robin([pair_steps(p) for p in range(npair)])
        return carry

    lax.fori_loop(0, nchunks, chunk, 0)

    @pl.when(tb == pl.num_programs(2) - 1)
    def _():
        for p in range(npair):
            so_ref[0, 2 * p] = st_ref[p, 0:n, 0:n]
            so_ref[0, 2 * p + 1] = st_ref[p, n:w, n:w]


def _rwkv(acts, vecs, s0, row_off, nseq, t, dst):
    m, d = acts[0].shape
    heads = d // RW_HEAD
    pairs = heads // 2
    npair = _pick(pairs, (RW_PAIRS_PER_STEP, 4, 2, 1))
    c = min(RW_CHUNK, t)
    tblk = _pick(t, (256, 128, 64, 32))
    nblk = t // tblk
    off = row_off // tblk
    wblk = npair * 2 * RW_HEAD
    aspec = pl.BlockSpec((tblk, wblk), lambda s, h, b: (off + s * nblk + b, h))
    vspec = pl.BlockSpec((1, wblk), lambda s, h, b: (0, h))
    sspec = pl.BlockSpec((1, 2 * npair, RW_HEAD, RW_HEAD), lambda s, h, b: (s, h, 0, 0))
    body = functools.partial(_rwkv_body, c=c, nchunks=tblk // c, npair=npair)
    return _group_call(
        body, (nseq, pairs // npair, nblk),
        [aspec] * 6 + [vspec] * 5 + [sspec],
        list(acts) + [x.reshape(1, d) for x in vecs] + [s0],
        [aspec, sspec],
        [jax.ShapeDtypeStruct((m, d), bf16), jax.ShapeDtypeStruct((nseq, heads, RW_HEAD, RW_HEAD), f32)],
        dst, [pltpu.VMEM((npair, 2 * RW_HEAD, 2 * RW_HEAD), f32)], ("parallel", "parallel", "arbitrary"),
        "rwkv7_scan")


def _conv_body(hg_ref, halo_ref, hu_ref, c0_ref, cw_ref, cb_ref, *rest, ndst):
    o_ref, cn_ref = rest[ndst:]
    tb = pl.program_id(2)
    x = hg_ref[...].astype(f32)
    rows = x.shape[0]
    c0 = c0_ref[0]
    halo = halo_ref[...].astype(f32)
    hm1 = jnp.where(tb == 0, c0[1:2, :], halo[7:8, :])
    hm2 = jnp.where(tb == 0, c0[0:1, :], halo[6:7, :])
    row = lax.broadcasted_iota(jnp.int32, x.shape, 0)
    p1 = jnp.where(row == 0, hm1, pltpu.roll(x, 1, axis=0))
    p2 = jnp.where(row == 0, hm2, jnp.where(row == 1, hm1, pltpu.roll(x, 2, axis=0)))
    cw = cw_ref[...]
    hc = cb_ref[...] + (cw[0:1, :] * p2 + cw[1:2, :] * p1 + cw[2:3, :] * x)
    act = 0.5 * hc * (1.0 + lax.erf(hc * 0.7071067811865476))
    o_ref[...] = (act * hu_ref[...].astype(f32)).astype(o_ref.dtype)

    @pl.when(tb == pl.num_programs(2) - 1)
    def _():
        cn_ref[0] = x[rows - 2:rows, :]


def _conv_gate(up, c0, cw, cb, row_off, nseq, t, dst):
    m = up.shape[0]
    f = up.shape[1] // 2
    tblk = _pick(t, (512, 256, 128, 64, 32))
    half = f // 2
    tf = half if (tblk <= 64 and half % 128 == 0) else _pick(f, (512, 256, 128))
    nblk = t // tblk
    nf = f // tf
    off = row_off // tblk
    hb = tblk // 8
    return _group_call(
        _conv_body, (nseq, nf, nblk),
        [pl.BlockSpec((tblk, tf), lambda s, j, b: (off + s * nblk + b, j)),
         pl.BlockSpec((8, tf), lambda s, j, b: (jnp.maximum((off + s * nblk + b) * hb - 1, 0), j)),
         pl.BlockSpec((tblk, tf), lambda s, j, b: (off + s * nblk + b, nf + j)),
         pl.BlockSpec((1, 2, tf), lambda s, j, b: (s, 0, j)),
         pl.BlockSpec((3, tf), lambda s, j, b: (0, j)),
         pl.BlockSpec((1, tf), lambda s, j, b: (0, j))],
        [up, up, up, c0, cw, cb.reshape(1, f)],
        [pl.BlockSpec((tblk, tf), lambda s, j, b: (off + s * nblk + b, j)),
         pl.BlockSpec((1, 2, tf), lambda s, j, b: (s, 0, j))],
        [jax.ShapeDtypeStruct((m, f), bf16), jax.ShapeDtypeStruct((nseq, 2, f), f32)],
        dst, [], ("parallel", "parallel", "arbitrary"), "convffn_gate")


def _first(accs, tiles, rows):
    return accs[0]


def _resid(accs, tiles, rows):
    return tiles[0] + accs[0]


def _bias(accs, tiles, rows):
    return accs[0] + rows[0]


def _ple(accs, tiles, rows):
    return tiles[0] + jax.nn.sigmoid(accs[0]) * accs[1]


def _tanh(accs, tiles, rows):
    return jnp.tanh(accs[0])


def _sigm(accs, tiles, rows):
    return jax.nn.sigmoid(accs[0])


def kernel(x_prompt, x_sample, p_prompt, p_sample, state_hgrn, state_rwkv, state_shift, state_ffn_conv, norm_mix, norm_ffn, norm_ple, norm_final, hg_w_in, hg_lb_logits, hg_gnorm, hg_w_o, rw_mu, rw_w_rkv, rw_w0, rw_w1, rw_w2, rw_a0, rw_a1, rw_a2, rw_g1, rw_g2, rw_k_k, rw_k_a, rw_r_k, rw_lnx_w, rw_lnx_b, rw_w_o, ffn_w_up, ffn_conv_w, ffn_conv_b, ffn_w_down, ple_w_proj, ple_w_gate):
    bp, tp, d = x_prompt.shape
    bs, ts, _ = x_sample.shape
    depth = norm_mix.shape[0]
    mp, ms = bp * tp, bs * ts
    groups = ((0, bp, tp), (mp, bs, ts))
    f_ff = ffn_conv_b.shape[1]
    cast = lambda w: w.astype(bf16)

    h = jnp.concatenate([x_prompt.reshape(mp, d), x_sample.reshape(ms, d)], axis=0)
    p_all = jnp.concatenate([p_prompt.reshape(depth, mp, -1), p_sample.reshape(depth, ms, -1)], axis=1).astype(bf16)
    lb_all = jnp.cumsum(jax.nn.softmax(hg_lb_logits.astype(f32), axis=0), axis=0)

    hg_states = [[], []]
    rw_states = [[], []]
    sh_states = [[], []]
    cv_states = [[], []]
    for i in range(depth):
        j = i // 2
        if i % 2 == 0:
            xn = _rmsnorm(h, norm_mix[i], bf16)
            z = _mm([(xn, cast(hg_w_in[j]))], _first, f32, name="hgrn_in")
            mix_in = None
            for gi, (off, nseq, t) in enumerate(groups):
                s0 = jnp.zeros((nseq, d // HG_HEAD, HG_HEAD, HG_HEAD), f32) if gi == 0 else state_hgrn[j].astype(f32)
                mix_in, s_new = _hgrn(z, lb_all[j], hg_gnorm[j], s0, off, nseq, t, None if mix_in is None else [mix_in])
                hg_states[gi].append(s_new)
            h = _mm([(mix_in, cast(hg_w_o[j]))], _resid, f32, tiles=(h,), name="hgrn_out")
        else:
            xn = _rmsnorm(h, norm_mix[i], f32)
            mixes = None
            for gi, (off, nseq, t) in enumerate(groups):
                sh0 = jnp.zeros((nseq, d), f32) if gi == 0 else state_shift[j]
                mixes = _shift_mix(xn, sh0, rw_mu[j], off, nseq, t, mixes)
                sh_states[gi].append(xn[off:off + nseq * t].reshape(nseq, t, d)[:, -1])
            xr, xw, xk, xv, xa, xg = mixes
            wr, wk, wv = (cast(rw_w_rkv[j][:, q * d:(q + 1) * d]) for q in range(3))
            r = _mm([(xr, wr)], _first, f32, name="rwkv_r")
            k = _mm([(xk, wk)], _first, f32, name="rwkv_k")
            v = _mm([(xv, wv)], _first, f32, name="rwkv_v")
            w_lin = _mm([(_mm([(xw, cast(rw_w1[j]))], _tanh, bf16, name="rwkv_w1"), cast(rw_w2[j]))],
                        _bias, f32, rows=(rw_w0[j],), name="rwkv_w2")
            a_lin = _mm([(_mm([(xa, cast(rw_a1[j]))], _first, bf16, name="rwkv_a1"), cast(rw_a2[j]))],
                        _bias, f32, rows=(rw_a0[j],), name="rwkv_a2")
            g = _mm([(_mm([(xg, cast(rw_g1[j]))], _sigm, bf16, name="rwkv_g1"), cast(rw_g2[j]))],
                    _first, f32, name="rwkv_g2")
            vecs = (rw_k_k[j], rw_k_a[j], rw_r_k[j].reshape(d), rw_lnx_w[j], rw_lnx_b[j])
            mix_in = None
            for gi, (off, nseq, t) in enumerate(groups):
                s0 = jnp.zeros((nseq, d // RW_HEAD, RW_HEAD, RW_HEAD), f32) if gi == 0 else state_rwkv[j].astype(f32)
                mix_in, s_new = _rwkv((r, k, v, w_lin, a_lin, g), vecs, s0, off, nseq, t,
                                      None if mix_in is None else [mix_in])
                rw_states[gi].append(s_new)
            h = _mm([(mix_in, cast(rw_w_o[j]))], _resid, f32, tiles=(h,), name="rwkv_out")

        xn = _rmsnorm(h, norm_ffn[i], bf16)
        up = _mm([(xn, cast(ffn_w_up[i]))], _first, bf16, name="ffn_up")
        act = None
        for gi, (off, nseq, t) in enumerate(groups):
            c0 = jnp.zeros((nseq, 2, f_ff), f32) if gi == 0 else state_ffn_conv[i]
            act, c_new = _conv_gate(up, c0, ffn_conv_w[i], ffn_conv_b[i], off, nseq, t, None if act is None else [act])
            cv_states[gi].append(c_new)
        h = _mm([(act, cast(ffn_w_down[i]))], _resid, f32, tiles=(h,), tm=512, tn=256, name="ffn_down")

        xn = _rmsnorm(h, norm_ple[i], bf16)
        h = _mm([(xn, cast(ple_w_gate[i])), (p_all[i], cast(ple_w_proj[i]))], _ple, f32, tiles=(h,), name="ple")

    y = _rmsnorm(h, norm_final, f32)
    y_prompt = y[:mp].reshape(bp, tp, d)
    y_sample = y[mp:].reshape(bs, ts, d)
    st = lambda xs: jnp.stack(xs)
    return (y_prompt, y_sample,
            st(hg_states[0]), st(rw_states[0]), st(sh_states[0]), st(cv_states[0]),
            st(hg_states[1]), st(rw_states[1]), st(sh_states[1]), st(cv_states[1]))
```

```python
import functools

import numpy as np
import jax
import jax.numpy as jnp
from jax import lax
from jax.experimental import pallas as pl
from jax.experimental.pallas import tpu as pltpu

f32 = jnp.float32
bf16 = jnp.bfloat16

RMS_EPS = 1e-6
GN_EPS = 64e-5
HG_HEAD = 128
HG_CHUNK = 128
HG_HEADS_PER_STEP = 4
RW_HEAD = 64
RW_CHUNK = 64
RW_PAIRS_PER_STEP = 8
VMEM_LIMIT = 50 * 1024 * 1024


def _pick(n, prefs):
    for p in prefs:
        if n % p == 0:
            return p
    return n


def _params(sem):
    return pltpu.CompilerParams(dimension_semantics=sem, vmem_limit_bytes=VMEM_LIMIT)


def _nn(a, b):
    return jnp.dot(a.astype(bf16), b.astype(bf16), preferred_element_type=f32)


def _nt(a, b):
    return lax.dot_general(a.astype(bf16), b.astype(bf16), (((1,), (1,)), ((), ())), preferred_element_type=f32)


def _tn(a, b):
    return lax.dot_general(a.astype(bf16), b.astype(bf16), (((0,), (0,)), ((), ())), preferred_element_type=f32)


def _d(a, b):
    return jnp.dot(a, b, preferred_element_type=f32)


def _dnt(a, b):
    return lax.dot_general(a, b, (((1,), (1,)), ((), ())), preferred_element_type=f32)


def _dtn(a, b):
    return lax.dot_general(a, b, (((0,), (0,)), ((), ())), preferred_element_type=f32)


def _split3(x):
    hi = x.astype(bf16)
    r1 = x - hi.astype(f32)
    mid = r1.astype(bf16)
    lo = (r1 - mid.astype(f32)).astype(bf16)
    return jnp.concatenate([hi, mid, lo], axis=1)


def _sum3(d3, w):
    return d3[:, 0:w] + (d3[:, w:2 * w] + d3[:, 2 * w:3 * w])


def _round_robin(gens):
    for _ in zip(*gens):
        pass


def _group_call(body, grid, in_specs, ins, out_specs, out_shapes, dst, scratch, sem, name):
    in_specs = list(in_specs)
    ins = list(ins)
    aliases = {}
    if dst is not None:
        for q, dq in enumerate(dst):
            aliases[len(ins)] = q
            in_specs.append(pl.BlockSpec(memory_space=pl.ANY))
            ins.append(dq)
    return pl.pallas_call(
        functools.partial(body, ndst=0 if dst is None else len(dst)),
        grid=grid, in_specs=in_specs, out_specs=out_specs, out_shape=out_shapes,
        scratch_shapes=scratch, input_output_aliases=aliases,
        compiler_params=_params(sem), name=name,
    )(*ins)


def _rms_body(x_ref, g_ref, o_ref):
    x = x_ref[...]
    ms = jnp.mean(x * x, axis=-1, keepdims=True)
    o_ref[...] = (x * lax.rsqrt(ms + RMS_EPS) * g_ref[...]).astype(o_ref.dtype)


def _rmsnorm(x, g, out_dtype, row_off=0, nrows=None):
    d = x.shape[1]
    m = x.shape[0] if nrows is None else nrows
    tm = _pick(m, (256, 128, 64, 32, 16, 8))
    off = row_off // tm
    return pl.pallas_call(
        _rms_body,
        grid=(m // tm,),
        in_specs=[pl.BlockSpec((tm, d), lambda i: (off + i, 0)), pl.BlockSpec((1, d), lambda i: (0, 0))],
        out_specs=pl.BlockSpec((tm, d), lambda i: (i, 0)),
        out_shape=jax.ShapeDtypeStruct((m, d), out_dtype),
        compiler_params=_params(("parallel",)),
        name="rmsnorm",
    )(x, g.reshape(1, d))


def _mm_body(*refs, nd, nt, nr, epi):
    dots = refs[: 2 * nd]
    tiles = refs[2 * nd: 2 * nd + nt]
    rows = refs[2 * nd + nt: 2 * nd + nt + nr]
    o_ref = refs[-1]
    accs = [jnp.dot(dots[2 * i][...], dots[2 * i + 1][...], preferred_element_type=f32) for i in range(nd)]
    o_ref[...] = epi(accs, [t[...] for t in tiles], [r[...] for r in rows]).astype(o_ref.dtype)


def _mm(dots, epi, out_dtype, tiles=(), rows=(), tm=1024, tn=512, ncols=None, name="mm"):
    m = dots[0][0].shape[0]
    n = dots[0][1].shape[1] if ncols is None else ncols
    tm = _pick(m, (tm, 512, 256, 128, 64, 32, 16, 8))
    tn = _pick(n, (tn, 256, 128))
    ins, specs = [], []
    for x, w in dots:
        k = x.shape[1]
        ins += [x, w]
        specs += [pl.BlockSpec((tm, k), lambda i, j: (i, 0)), pl.BlockSpec((k, tn), lambda i, j: (0, j))]
    for t in tiles:
        ins.append(t)
        specs.append(pl.BlockSpec((tm, tn), lambda i, j: (i, j)))
    for r in rows:
        ins.append(r.reshape(1, n))
        specs.append(pl.BlockSpec((1, tn), lambda i, j: (0, j)))
    body = functools.partial(_mm_body, nd=len(dots), nt=len(tiles), nr=len(rows), epi=epi)
    return pl.pallas_call(
        body,
        grid=(m // tm, n // tn),
        in_specs=specs,
        out_specs=pl.BlockSpec((tm, tn), lambda i, j: (i, j)),
        out_shape=jax.ShapeDtypeStruct((m, n), out_dtype),
        compiler_params=_params(("parallel", "parallel")),
        name=name,
    )(*ins)


def _hgrn_consts(c):
    idx = np.arange(c)
    mats = [(idx[:, None] >= idx[None, :])]
    masks = [np.eye(c, dtype=bool)]
    b = 1
    while b < c:
        blk = idx // (2 * b)
        second = (idx % (2 * b)) >= b
        bnd = blk * 2 * b + b - 1
        lq = (idx[None, :] > bnd[:, None]) & (idx[None, :] <= idx[:, None]) & second[:, None]
        lk = (idx[None, :] > idx[:, None]) & (idx[None, :] <= bnd[:, None]) & (~second)[:, None]
        mats.append(lq | lk)
        masks.append((blk[:, None] == blk[None, :]) & second[:, None] & (~second)[None, :])
        b *= 2
    lmat = np.concatenate(mats, axis=0).astype(np.float32)
    return jnp.asarray(lmat, dtype=bf16), jnp.asarray(np.stack(masks).astype(np.float32)), len(mats) - 1


def _hgrn_body(q_ref, f_ref, i_ref, g_ref, lb_ref, gn_ref, s0_ref, l_ref, mask_ref, *rest, c, nl, nchunks, nh, ndst):
    o_ref, so_ref, st_ref = rest[ndst:]
    tb = pl.program_id(2)
    w = HG_HEAD

    @pl.when(tb == 0)
    def _():
        for hh in range(nh):
            st_ref[hh] = s0_ref[0, hh].T

    gn = gn_ref[...]

    def chunk(ci, carry):
        r0 = pl.multiple_of(ci * c, c)
        rows = pl.ds(r0, c)
        def head_steps(hh):
            cols = slice(hh * w, (hh + 1) * w)
            lb = lb_ref[:, cols]
            q = q_ref[rows, cols]
            fg = lb + (1.0 - lb) * jax.nn.sigmoid(f_ref[rows, cols])
            gl = jnp.log(fg)
            kk = 1.0 - fg
            qs = q * jax.nn.sigmoid(q)
            v = i_ref[rows, cols]
            d = _sum3(jnp.dot(l_ref[...], _split3(gl), preferred_element_type=f32), w)
            a = mask_ref[0] * _nt(qs, kk)
            yield
            gcum = d[0:c]
            gend = gcum[c - 1:c, :]
            st = st_ref[hh]
            oi = _nt(qs * jnp.exp(gcum), st)
            st_ref[hh] = st * jnp.exp(gend) + _tn(v, kk * jnp.exp(gend - gcum))
            for l in range(nl):
                e = jnp.exp(d[(1 + l) * c:(2 + l) * c])
                a = a + mask_ref[1 + l] * _nt(qs * e, kk * e)
            yield
            o = _nn(a, v) + oi
            yield
            ms = jnp.mean(o * o, axis=-1, keepdims=True)
            on = o * lax.rsqrt(ms + RMS_EPS) * gn
            o_ref[rows, cols] = (on * jax.nn.sigmoid(g_ref[rows, cols])).astype(o_ref.dtype)
            yield

        _round_robin([head_steps(hh) for hh in range(nh)])
        return carry

    lax.fori_loop(0, nchunks, chunk, 0)

    @pl.when(tb == pl.num_programs(2) - 1)
    def _():
        for hh in range(nh):
            so_ref[0, hh] = st_ref[hh].T


def _hgrn(z, lb, gnorm, s0, row_off, nseq, t, dst):
    m = z.shape[0]
    d = z.shape[1] // 4
    heads = d // HG_HEAD
    c = min(HG_CHUNK, t)
    nh = _pick(heads, (HG_HEADS_PER_STEP, 2, 1))
    tblk = _pick(t, (512, 256, 128, 64, 32))
    nblk = t // tblk
    off = row_off // tblk
    hgrp = heads // nh
    wblk = nh * HG_HEAD
    lmat, masks, nl = _hgrn_consts(c)

    def zspec(sec):
        return pl.BlockSpec((tblk, wblk), lambda s, h, b: (off + s * nblk + b, sec * hgrp + h))

    sspec = pl.BlockSpec((1, nh, HG_HEAD, HG_HEAD), lambda s, h, b: (s, h, 0, 0))
    body = functools.partial(_hgrn_body, c=c, nl=nl, nchunks=tblk // c, nh=nh)
    return _group_call(
        body, (nseq, hgrp, nblk),
        [zspec(0), zspec(1), zspec(2), zspec(3),
         pl.BlockSpec((1, wblk), lambda s, h, b: (0, h)),
         pl.BlockSpec((1, HG_HEAD), lambda s, h, b: (0, 0)),
         sspec,
         pl.BlockSpec(lmat.shape, lambda s, h, b: (0, 0)),
         pl.BlockSpec(masks.shape, lambda s, h, b: (0, 0, 0))],
        [z, z, z, z, lb.reshape(1, d), gnorm.reshape(1, HG_HEAD), s0, lmat, masks],
        [pl.BlockSpec((tblk, wblk), lambda s, h, b: (off + s * nblk + b, h)), sspec],
        [jax.ShapeDtypeStruct((m, d), bf16), jax.ShapeDtypeStruct((nseq, heads, HG_HEAD, HG_HEAD), f32)],
        dst, [pltpu.VMEM((nh, HG_HEAD, HG_HEAD), f32)], ("parallel", "parallel", "arbitrary"), "hgrn2_scan")


def _shift_body(x_ref, halo_ref, sh_ref, mu_ref, *rest, ndst):
    o_refs = rest[ndst:]
    tb = pl.program_id(2)
    x = x_ref[...]
    first = jnp.where(tb == 0, sh_ref[0], halo_ref[7:8, :])
    row = lax.broadcasted_iota(jnp.int32, x.shape, 0)
    xx = jnp.where(row == 0, first, pltpu.roll(x, 1, axis=0)) - x
    for j, o_ref in enumerate(o_refs):
        o_ref[...] = (x + xx * mu_ref[j:j + 1, :]).astype(o_ref.dtype)


def _shift_mix(xn, shift0, mu, row_off, nseq, t, dst):
    m, d = xn.shape
    tblk = _pick(t, (512, 256, 128, 64, 32))
    td = _pick(d, (1024, 512, 256, 128))
    nblk = t // tblk
    off = row_off // tblk
    hb = tblk // 8
    return _group_call(
        _shift_body, (nseq, d // td, nblk),
        [pl.BlockSpec((tblk, td), lambda s, j, b: (off + s * nblk + b, j)),
         pl.BlockSpec((8, td), lambda s, j, b: (jnp.maximum((off + s * nblk + b) * hb - 1, 0), j)),
         pl.BlockSpec((1, 1, td), lambda s, j, b: (s, 0, j)),
         pl.BlockSpec((6, td), lambda s, j, b: (0, j))],
        [xn, xn, shift0.reshape(nseq, 1, d), mu],
        [pl.BlockSpec((tblk, td), lambda s, j, b: (off + s * nblk + b, j))] * 6,
        [jax.ShapeDtypeStruct((m, d), bf16)] * 6,
        dst, [], ("parallel", "parallel", "arbitrary"), "rwkv_shift_mix")


def _rwkv_body(r_ref, k_ref, v_ref, w_ref, a_ref, g_ref, kk_ref, ka_ref, rk_ref, lnw_ref, lnb_ref, s0_ref, *rest,
               c, nchunks, npair, ndst):
    o_ref, so_ref, st_ref = rest[ndst:]
    tb = pl.program_id(2)
    n = RW_HEAD
    w = 2 * n
    c2 = 2 * c
    i32 = jnp.int32

    @pl.when(tb == 0)
    def _():
        st_ref[...] = jnp.zeros(st_ref.shape, f32)
        for p in range(npair):
            st_ref[p, 0:n, 0:n] = s0_ref[0, 2 * p]
            st_ref[p, n:w, n:w] = s0_ref[0, 2 * p + 1]

    ri = lax.broadcasted_iota(i32, (c2, c2), 0)
    cj = lax.broadcasted_iota(i32, (c2, c2), 1)
    same = (ri // c) == (cj // c)
    low_bd = (same & (ri >= cj)).astype(f32)
    slow_bd = (same & (ri > cj)).astype(f32)
    eye = (ri == cj).astype(f32)
    lvl = []
    b = 1
    while b < c:
        lvl.append(((ri // (2 * b) == cj // (2 * b)) & (ri % (2 * b) >= b) & (cj % (2 * b) < b)).astype(f32))
        b *= 2
    lvl_b = [m.astype(bf16) for m in lvl]
    lowc = (lax.broadcasted_iota(i32, (c, c), 0) >= lax.broadcasted_iota(i32, (c, c), 1)).astype(bf16)
    m0 = lax.broadcasted_iota(i32, (c, w), 1) < n
    hmask = (lax.broadcasted_iota(i32, (c2, w), 0) // c) == (lax.broadcasted_iota(i32, (c2, w), 1) // n)

    lane0 = m0.astype(bf16)
    lane1 = 1.0 - lane0

    def stack(x_b):
        return jnp.concatenate([x_b * lane0, x_b * lane1], axis=0)

    def halfsums(x):
        s0_ = jnp.sum(jnp.where(m0, x, 0.0), axis=-1, keepdims=True)
        s1_ = jnp.sum(jnp.where(m0, 0.0, x), axis=-1, keepdims=True)
        return s0_, s1_

    def chunk(ci, carry):
        r0 = pl.multiple_of(ci * c, c)
        rows = pl.ds(r0, c)
        def pair_steps(p):
            cols = slice(p * w, (p + 1) * w)
            r = r_ref[rows, cols]
            k = k_ref[rows, cols]
            v = v_ref[rows, cols]
            lw = -jnp.exp(-jax.nn.softplus(-w_ref[rows, cols]) - 0.5)
            a = jax.nn.sigmoid(a_ref[rows, cols])
            kkp = k * kk_ref[:, cols]
            k2 = k * (1.0 + (a - 1.0) * ka_ref[:, cols])
            cum = _sum3(jnp.dot(lowc, _split3(lw), preferred_element_type=f32), w)
            yield
            cend = cum[c - 1:c, :]
            e_incl = jnp.exp(cum)
            e_inv = jnp.exp(-cum)
            e_end = jnp.exp(cend - cum)
            gam = jnp.exp(cend)
            n0, n1 = halfsums(kkp * kkp)
            rn0 = 1.0 / jnp.maximum(jnp.sqrt(n0), 1e-12)
            rn1 = 1.0 / jnp.maximum(jnp.sqrt(n1), 1e-12)
            kkn = kkp * jnp.where(m0, rn0, rn1)
            ka_ = kkn * a
            kkt_b = stack((kkn * jnp.exp(cum - lw)).astype(bf16))
            bh_b = stack((ka_ * e_inv).astype(bf16))
            bb_b = stack((ka_ * e_end).astype(bf16))
            rt_b = stack((r * e_incl).astype(bf16))
            kh_b = stack((k2 * e_inv).astype(bf16))
            kb_b = stack((k2 * e_end).astype(bf16))
            v_b = stack(v.astype(bf16))
            gram = _dnt(jnp.concatenate([kkt_b, rt_b], axis=0), jnp.concatenate([kh_b, bh_b], axis=0))
            yield
            mb = slow_bd * gram[0:c2, c2:2 * c2]
            mk_b = (slow_bd * gram[0:c2, 0:c2]).astype(bf16)
            pk_b = (low_bd * gram[c2:2 * c2, 0:c2]).astype(bf16)
            pb_b = (low_bd * gram[c2:2 * c2, c2:2 * c2]).astype(bf16)
            mkv = _d(mk_b, v_b)
            pkv = _d(pk_b, v_b)
            vk = _dtn(v_b, kb_b)
            tm = eye - lvl[0] * mb
            mb_b = mb.astype(bf16)
            for l in range(1, len(lvl)):
                tm_b = tm.astype(bf16)
                tc = _d(tm_b, mb_b * lvl_b[l])
                yield
                tm = tm - _d(tc.astype(bf16), tm_b)
                yield
            wu = _d(tm.astype(bf16), jnp.concatenate([kkt_b, mkv.astype(bf16)], axis=1))
            yield
            s = st_ref[p]
            s_b = s.astype(bf16)
            u_b = (_dnt(wu[:, 0:w].astype(bf16), s_b) + wu[:, w:2 * w]).astype(bf16)
            yr = _dnt(rt_b, s_b) + pkv
            yield
            y_s = yr - _d(pb_b, u_b)
            st_ref[p] = s * gam + vk - _dtn(u_b, bb_b)
            yield
            mean = jnp.sum(y_s, axis=-1, keepdims=True) * (1.0 / n)
            yc = jnp.where(hmask, y_s - mean, 0.0)
            var = jnp.sum(yc * yc, axis=-1, keepdims=True) * (1.0 / n)
            yn_s = yc * lax.rsqrt(var + GN_EPS)
            yn = (yn_s[0:c] + yn_s[c:c2]) * lnw_ref[:, cols] + lnb_ref[:, cols]
            b0, b1 = halfsums(r * k2 * rk_ref[:, cols])
            bonus = jnp.where(m0, b0, b1) * v
            o_ref[rows, cols] = ((yn + bonus) * g_ref[rows, cols]).astype(o_ref.dtype)
            yield

        _round_robin([pair_steps(p) for p in range(npair)])
        return carry

    lax.fori_loop(0, nchunks, chunk, 0)

    @pl.when(tb == pl.num_programs(2) - 1)
    def _():
        for p in range(npair):
            so_ref[0, 2 * p] = st_ref[p, 0:n, 0:n]
            so_ref[0, 2 * p + 1] = st_ref[p, n:w, n:w]


def _rwkv(acts, vecs, s0, row_off, nseq, t, dst):
    m, d = acts[0].shape
    heads = d // RW_HEAD
    pairs = heads // 2
    npair = _pick(pairs, (RW_PAIRS_PER_STEP, 4, 2, 1))
    c = min(RW_CHUNK, t)
    tblk = _pick(t, (256, 128, 64, 32))
    nblk = t // tblk
    off = row_off // tblk
    wblk = npair * 2 * RW_HEAD
    aspec = pl.BlockSpec((tblk, wblk), lambda s, h, b: (off + s * nblk + b, h))
    vspec = pl.BlockSpec((1, wblk), lambda s, h, b: (0, h))
    sspec = pl.BlockSpec((1, 2 * npair, RW_HEAD, RW_HEAD), lambda s, h, b: (s, h, 0, 0))
    body = functools.partial(_rwkv_body, c=c, nchunks=tblk // c, npair=npair)
    return _group_call(
        body, (nseq, pairs // npair, nblk),
        [aspec] * 6 + [vspec] * 5 + [sspec],
        list(acts) + [x.reshape(1, d) for x in vecs] + [s0],
        [aspec, sspec],
        [jax.ShapeDtypeStruct((m, d), bf16), jax.ShapeDtypeStruct((nseq, heads, RW_HEAD, RW_HEAD), f32)],
        dst, [pltpu.VMEM((npair, 2 * RW_HEAD, 2 * RW_HEAD), f32)], ("parallel", "parallel", "arbitrary"),
        "rwkv7_scan")


def _upconv_body(x_ref, wg_ref, wu_ref, c0p_ref, c0s_ref, cw_ref, cb_ref, o_ref, carry_ref, *,
                 n_ptiles, tiles_per_seq, ts, nsplit):
    i = pl.program_id(0)
    j = pl.program_id(1)
    tm = x_ref.shape[0]
    tf = o_ref.shape[1]
    th = tm // nsplit
    cw = cw_ref[...]
    row = lax.broadcasted_iota(jnp.int32, (th, tf), 0)

    def part(hidx, t, hm1, hm2):
        rows = slice(hidx * th, (hidx + 1) * th)
        x = x_ref[rows, :]
        hg = jnp.dot(x, wg_ref[...], preferred_element_type=f32)
        hu = jnp.dot(x, wu_ref[...], preferred_element_type=f32)
        p1 = jnp.where(t == 0, hm1, pltpu.roll(hg, 1, axis=0))
        p2 = jnp.where(t == 0, hm2, jnp.where(t == 1, hm1, pltpu.roll(hg, 2, axis=0)))
        hc = cb_ref[...] + (cw[0:1, :] * p2 + cw[1:2, :] * p1 + cw[2:3, :] * hg)
        act = 0.5 * hc * (1.0 + lax.erf(hc * 0.7071067811865476))
        o_ref[rows, :] = (act * hu).astype(o_ref.dtype)
        return hg[th - 2:th, :]

    @pl.when(i < n_ptiles)
    def _():
        @pl.when(i % tiles_per_seq == 0)
        def _():
            carry_ref[j] = c0p_ref[0]
        prev = carry_ref[j]
        for hidx in range(nsplit):
            prev = part(hidx, row, prev[1:2, :], prev[0:1, :])
        carry_ref[j] = prev

    @pl.when(i >= n_ptiles)
    def _():
        nsq = th // ts
        for hidx in range(nsplit):
            c0 = c0s_ref[hidx * nsq:(hidx + 1) * nsq]
            hm1 = jnp.broadcast_to(c0[:, 1:2, :], (nsq, ts, tf)).reshape(th, tf)
            hm2 = jnp.broadcast_to(c0[:, 0:1, :], (nsq, ts, tf)).reshape(th, tf)
            part(hidx, row % ts, hm1, hm2)


def _ffn_up_conv(xn, w_up, c0p, c0s, cw, cb, mp, tp, ts):
    m, k = xn.shape
    f = w_up.shape[1] // 2
    ms = m - mp
    tm = next(t for t in (1024, 512, 256, 128, 64, 32, 16, 8) if tp % t == 0 and ms % t == 0 and t % ts == 0)
    tf = _pick(f, (256, 128))
    nf = f // tf
    n_ptiles = mp // tm
    tiles_per_seq = tp // tm
    nsq = tm // ts
    bp = c0p.shape[0]
    body = functools.partial(_upconv_body, n_ptiles=n_ptiles, tiles_per_seq=tiles_per_seq, ts=ts, nsplit=1)
    return pl.pallas_call(
        body,
        grid=(m // tm, nf),
        in_specs=[pl.BlockSpec((tm, k), lambda i, j: (i, 0)),
                  pl.BlockSpec((k, tf), lambda i, j: (0, j)),
                  pl.BlockSpec((k, tf), lambda i, j: (0, nf + j)),
                  pl.BlockSpec((1, 2, tf), lambda i, j: (jnp.minimum(i // tiles_per_seq, bp - 1), 0, j)),
                  pl.BlockSpec((nsq, 2, tf), lambda i, j: (jnp.maximum(i - n_ptiles, 0), 0, j)),
                  pl.BlockSpec((3, tf), lambda i, j: (0, j)),
                  pl.BlockSpec((1, tf), lambda i, j: (0, j))],
        out_specs=pl.BlockSpec((tm, tf), lambda i, j: (i, j)),
        out_shape=jax.ShapeDtypeStruct((m, f), bf16),
        scratch_shapes=[pltpu.VMEM((nf, 2, tf), f32)],
        compiler_params=_params(("arbitrary", "arbitrary")),
        name="ffn_up_conv",
    )(xn, w_up, w_up, c0p, c0s, cw, cb.reshape(1, f))


def _first(accs, tiles, rows):
    return accs[0]


def _resid(accs, tiles, rows):
    return tiles[0] + accs[0]


def _bias(accs, tiles, rows):
    return accs[0] + rows[0]


def _ple(accs, tiles, rows):
    return tiles[0] + jax.nn.sigmoid(accs[0]) * accs[1]


def _tanh(accs, tiles, rows):
    return jnp.tanh(accs[0])


def _sigm(accs, tiles, rows):
    return jax.nn.sigmoid(accs[0])


def kernel(x_prompt, x_sample, p_prompt, p_sample, state_hgrn, state_rwkv, state_shift, state_ffn_conv, norm_mix, norm_ffn, norm_ple, norm_final, hg_w_in, hg_lb_logits, hg_gnorm, hg_w_o, rw_mu, rw_w_rkv, rw_w0, rw_w1, rw_w2, rw_a0, rw_a1, rw_a2, rw_g1, rw_g2, rw_k_k, rw_k_a, rw_r_k, rw_lnx_w, rw_lnx_b, rw_w_o, ffn_w_up, ffn_conv_w, ffn_conv_b, ffn_w_down, ple_w_proj, ple_w_gate):
    bp, tp, d = x_prompt.shape
    bs, ts, _ = x_sample.shape
    depth = norm_mix.shape[0]
    mp, ms = bp * tp, bs * ts
    groups = ((0, bp, tp), (mp, bs, ts))
    f_ff = ffn_conv_b.shape[1]
    cast = lambda w: w.astype(bf16)

    h = jnp.concatenate([x_prompt.reshape(mp, d), x_sample.reshape(ms, d)], axis=0)
    p_all = jnp.concatenate([p_prompt.reshape(depth, mp, -1), p_sample.reshape(depth, ms, -1)], axis=1).astype(bf16)
    lb_all = jnp.cumsum(jax.nn.softmax(hg_lb_logits.astype(f32), axis=0), axis=0)

    hg_states = [[], []]
    rw_states = [[], []]
    sh_states = [[], []]
    cv_states = [[], []]
    for i in range(depth):
        j = i // 2
        if i % 2 == 0:
            xn = _rmsnorm(h, norm_mix[i], bf16)
            z = _mm([(xn, cast(hg_w_in[j]))], _first, f32, name="hgrn_in")
            mix_in = None
            for gi, (off, nseq, t) in enumerate(groups):
                s0 = jnp.zeros((nseq, d // HG_HEAD, HG_HEAD, HG_HEAD), f32) if gi == 0 else state_hgrn[j].astype(f32)
                mix_in, s_new = _hgrn(z, lb_all[j], hg_gnorm[j], s0, off, nseq, t, None if mix_in is None else [mix_in])
                hg_states[gi].append(s_new)
            h = _mm([(mix_in, cast(hg_w_o[j]))], _resid, f32, tiles=(h,), name="hgrn_out")
        else:
            xn = _rmsnorm(h, norm_mix[i], f32)
            mixes = None
            for gi, (off, nseq, t) in enumerate(groups):
                sh0 = jnp.zeros((nseq, d), f32) if gi == 0 else state_shift[j]
                mixes = _shift_mix(xn, sh0, rw_mu[j], off, nseq, t, mixes)
                sh_states[gi].append(xn[off:off + nseq * t].reshape(nseq, t, d)[:, -1])
            xr, xw, xk, xv, xa, xg = mixes
            wr, wk, wv = (cast(rw_w_rkv[j][:, q * d:(q + 1) * d]) for q in range(3))
            r = _mm([(xr, wr)], _first, f32, name="rwkv_r")
            k = _mm([(xk, wk)], _first, f32, name="rwkv_k")
            v = _mm([(xv, wv)], _first, f32, name="rwkv_v")
            w_lin = _mm([(_mm([(xw, cast(rw_w1[j]))], _tanh, bf16, name="rwkv_w1"), cast(rw_w2[j]))],
                        _bias, f32, rows=(rw_w0[j],), name="rwkv_w2")
            a_lin = _mm([(_mm([(xa, cast(rw_a1[j]))], _first, bf16, name="rwkv_a1"), cast(rw_a2[j]))],
                        _bias, f32, rows=(rw_a0[j],), name="rwkv_a2")
            g = _mm([(_mm([(xg, cast(rw_g1[j]))], _sigm, bf16, name="rwkv_g1"), cast(rw_g2[j]))],
                    _first, f32, name="rwkv_g2")
            vecs = (rw_k_k[j], rw_k_a[j], rw_r_k[j].reshape(d), rw_lnx_w[j], rw_lnx_b[j])
            mix_in = None
            for gi, (off, nseq, t) in enumerate(groups):
                s0 = jnp.zeros((nseq, d // RW_HEAD, RW_HEAD, RW_HEAD), f32) if gi == 0 else state_rwkv[j].astype(f32)
                mix_in, s_new = _rwkv((r, k, v, w_lin, a_lin, g), vecs, s0, off, nseq, t,
                                      None if mix_in is None else [mix_in])
                rw_states[gi].append(s_new)
            h = _mm([(mix_in, cast(rw_w_o[j]))], _resid, f32, tiles=(h,), name="rwkv_out")

        xn = _rmsnorm(h, norm_ffn[i], bf16)
        w_up = cast(ffn_w_up[i])
        act = _ffn_up_conv(xn, w_up, jnp.zeros((bp, 2, f_ff), f32), state_ffn_conv[i].astype(f32),
                           ffn_conv_w[i], ffn_conv_b[i], mp, tp, ts)
        x_last = jnp.concatenate([xn[:mp].reshape(bp, tp, d)[:, tp - 2:].reshape(2 * bp, d),
                                  xn[mp:].reshape(bs, ts, d)[:, ts - 2:].reshape(2 * bs, d)], axis=0)
        c_new = _mm([(x_last, w_up)], _first, f32, ncols=f_ff, name="ffn_conv_state")
        cv_states[0].append(c_new[:2 * bp].reshape(bp, 2, f_ff))
        cv_states[1].append(c_new[2 * bp:].reshape(bs, 2, f_ff))
        h = _mm([(act, cast(ffn_w_down[i]))], _resid, f32, tiles=(h,), tm=512, tn=256, name="ffn_down")

        xn = _rmsnorm(h, norm_ple[i], bf16)
        h = _mm([(xn, cast(ple_w_gate[i])), (p_all[i], cast(ple_w_proj[i]))], _ple, f32, tiles=(h,), name="ple")

    y_prompt = _rmsnorm(h, norm_final, f32, 0, mp).reshape(bp, tp, d)
    y_sample = _rmsnorm(h, norm_final, f32, mp, ms).reshape(bs, ts, d)
    st = lambda xs: jnp.stack(xs)
    return (y_prompt, y_sample,
            st(hg_states[0]), st(rw_states[0]), st(sh_states[0]), st(cv_states[0]),
            st(hg_states[1]), st(rw_states[1]), st(sh_states[1]), st(cv_states[1]))
```

```python
import functools

import numpy as np
import jax
import jax.numpy as jnp
from jax import lax
from jax.experimental import pallas as pl
from jax.experimental.pallas import tpu as pltpu

f32 = jnp.float32
bf16 = jnp.bfloat16

RMS_EPS = 1e-6
GN_EPS = 64e-5
HG_HEAD = 128
HG_CHUNK = 128
HG_HEADS_PER_STEP = 4
RW_HEAD = 64
RW_CHUNK = 64
RW_PAIRS_PER_STEP = 8
VMEM_LIMIT = 56 * 1024 * 1024


def _pick(n, prefs):
    for p in prefs:
        if n % p == 0:
            return p
    return n


def _params(sem):
    return pltpu.CompilerParams(dimension_semantics=sem, vmem_limit_bytes=VMEM_LIMIT)


def _nn(a, b):
    return jnp.dot(a.astype(bf16), b.astype(bf16), preferred_element_type=f32)


def _nt(a, b):
    return lax.dot_general(a.astype(bf16), b.astype(bf16), (((1,), (1,)), ((), ())), preferred_element_type=f32)


def _tn(a, b):
    return lax.dot_general(a.astype(bf16), b.astype(bf16), (((0,), (0,)), ((), ())), preferred_element_type=f32)


def _d(a, b):
    return jnp.dot(a, b, preferred_element_type=f32)


def _dnt(a, b):
    return lax.dot_general(a, b, (((1,), (1,)), ((), ())), preferred_element_type=f32)


def _dtn(a, b):
    return lax.dot_general(a, b, (((0,), (0,)), ((), ())), preferred_element_type=f32)


def _split3(x):
    hi = x.astype(bf16)
    r1 = x - hi.astype(f32)
    mid = r1.astype(bf16)
    lo = (r1 - mid.astype(f32)).astype(bf16)
    return jnp.concatenate([hi, mid, lo], axis=1)


def _sum3(d3, w):
    return d3[:, 0:w] + (d3[:, w:2 * w] + d3[:, 2 * w:3 * w])


def _round_robin(gens):
    for _ in zip(*gens):
        pass


def _group_call(body, grid, in_specs, ins, out_specs, out_shapes, dst, scratch, sem, name):
    return pl.pallas_call(
        body,
        grid=grid, in_specs=list(in_specs) + [pl.BlockSpec(memory_space=pl.ANY)],
        out_specs=out_specs, out_shape=out_shapes,
        scratch_shapes=scratch, input_output_aliases={len(ins): 0},
        compiler_params=_params(sem), name=name,
    )(*ins, dst)


def _rms_body(x_ref, g_ref, o_ref):
    x = x_ref[...]
    ms = jnp.mean(x * x, axis=-1, keepdims=True)
    o_ref[...] = (x * lax.rsqrt(ms + RMS_EPS) * g_ref[...]).astype(o_ref.dtype)


def _rmsnorm(x, g, out_dtype, row_off=0, nrows=None):
    d = x.shape[1]
    m = x.shape[0] if nrows is None else nrows
    tm = _pick(m, (256, 128, 64, 32, 16, 8))
    off = row_off // tm
    return pl.pallas_call(
        _rms_body,
        grid=(m // tm,),
        in_specs=[pl.BlockSpec((tm, d), lambda i: (off + i, 0)), pl.BlockSpec((1, d), lambda i: (0, 0))],
        out_specs=pl.BlockSpec((tm, d), lambda i: (i, 0)),
        out_shape=jax.ShapeDtypeStruct((m, d), out_dtype),
        compiler_params=_params(("parallel",)),
        name="rmsnorm",
    )(x, g.reshape(1, d))


def _mm_body(*refs, nd, nt, nr, epi):
    dots = refs[: 2 * nd]
    tiles = refs[2 * nd: 2 * nd + nt]
    rows = refs[2 * nd + nt: 2 * nd + nt + nr]
    o_ref = refs[-1]
    accs = [jnp.dot(dots[2 * i][...], dots[2 * i + 1][...], preferred_element_type=f32) for i in range(nd)]
    o_ref[...] = epi(accs, [t[...] for t in tiles], [r[...] for r in rows]).astype(o_ref.dtype)


def _mm(dots, epi, out_dtype, tiles=(), rows=(), tm=1024, tn=1024, ncols=None, col_off=0, name="mm"):
    m = dots[0][0].shape[0]
    n = dots[0][1].shape[1] if ncols is None else ncols
    tm = _pick(m, (tm, 512, 256, 128, 64, 32, 16, 8))
    tn = _pick(n, (tn, 512, 256, 128))
    assert col_off % tn == 0
    joff = col_off // tn
    ins, specs = [], []
    for x, w in dots:
        k = x.shape[1]
        ins += [x, w]
        specs += [pl.BlockSpec((tm, k), lambda i, j: (i, 0)), pl.BlockSpec((k, tn), lambda i, j: (0, joff + j))]
    for t in tiles:
        ins.append(t)
        specs.append(pl.BlockSpec((tm, tn), lambda i, j: (i, j)))
    for r in rows:
        ins.append(r.reshape(1, n))
        specs.append(pl.BlockSpec((1, tn), lambda i, j: (0, j)))
    body = functools.partial(_mm_body, nd=len(dots), nt=len(tiles), nr=len(rows), epi=epi)
    return pl.pallas_call(
        body,
        grid=(m // tm, n // tn),
        in_specs=specs,
        out_specs=pl.BlockSpec((tm, tn), lambda i, j: (i, j)),
        out_shape=jax.ShapeDtypeStruct((m, n), out_dtype),
        compiler_params=_params(("parallel", "parallel")),
        name=name,
    )(*ins)


def _hgrn_consts(c):
    idx = np.arange(c)
    mats = [(idx[:, None] >= idx[None, :])]
    masks = [np.eye(c, dtype=bool)]
    b = 1
    while b < c:
        blk = idx // (2 * b)
        second = (idx % (2 * b)) >= b
        bnd = blk * 2 * b + b - 1
        lq = (idx[None, :] > bnd[:, None]) & (idx[None, :] <= idx[:, None]) & second[:, None]
        lk = (idx[None, :] > idx[:, None]) & (idx[None, :] <= bnd[:, None]) & (~second)[:, None]
        mats.append(lq | lk)
        masks.append((blk[:, None] == blk[None, :]) & second[:, None] & (~second)[None, :])
        b *= 2
    lmat = np.concatenate(mats, axis=0).astype(np.float32)
    return jnp.asarray(lmat, dtype=bf16), jnp.asarray(np.stack(masks).astype(np.float32)), len(mats) - 1


def _hgrn_body(q_ref, f_ref, i_ref, g_ref, lb_ref, gn_ref, s0_ref, l_ref, mask_ref, dst_ref, o_ref, so_ref, st_ref, *,
               c, nl, nchunks, nh):
    del dst_ref
    tb = pl.program_id(2)
    w = HG_HEAD

    @pl.when(tb == 0)
    def _():
        for hh in range(nh):
            st_ref[hh] = s0_ref[0, hh].T

    gn = gn_ref[...]

    def chunk(ci, carry):
        r0 = pl.multiple_of(ci * c, c)
        rows = pl.ds(r0, c)
        def head_steps(hh):
            cols = slice(hh * w, (hh + 1) * w)
            lb = lb_ref[:, cols]
            q = q_ref[rows, cols]
            fg = lb + (1.0 - lb) * jax.nn.sigmoid(f_ref[rows, cols])
            gl = jnp.log(fg)
            kk = 1.0 - fg
            qs = q * jax.nn.sigmoid(q)
            v = i_ref[rows, cols]
            d = _sum3(jnp.dot(l_ref[...], _split3(gl), preferred_element_type=f32), w)
            a = mask_ref[0] * _nt(qs, kk)
            yield
            gcum = d[0:c]
            gend = gcum[c - 1:c, :]
            st = st_ref[hh]
            oi = _nt(qs * jnp.exp(gcum), st)
            st_ref[hh] = st * jnp.exp(gend) + _tn(v, kk * jnp.exp(gend - gcum))
            for l in range(nl):
                e = jnp.exp(d[(1 + l) * c:(2 + l) * c])
                a = a + mask_ref[1 + l] * _nt(qs * e, kk * e)
            yield
            o = _nn(a, v) + oi
            yield
            ms = jnp.mean(o * o, axis=-1, keepdims=True)
            on = o * lax.rsqrt(ms + RMS_EPS) * gn
            o_ref[rows, cols] = (on * jax.nn.sigmoid(g_ref[rows, cols])).astype(o_ref.dtype)
            yield

        _round_robin([head_steps(hh) for hh in range(nh)])
        return carry

    lax.fori_loop(0, nchunks, chunk, 0)

    @pl.when(tb == pl.num_programs(2) - 1)
    def _():
        for hh in range(nh):
            so_ref[0, hh] = st_ref[hh].T


def _hgrn(z, lb, gnorm, s0, row_off, nseq, t, dst):
    m = z.shape[0]
    d = z.shape[1] // 4
    heads = d // HG_HEAD
    c = min(HG_CHUNK, t)
    nh = _pick(heads, (HG_HEADS_PER_STEP, 2, 1))
    tblk = _pick(t, (512, 256, 128, 64, 32))
    nblk = t // tblk
    off = row_off // tblk
    hgrp = heads // nh
    wblk = nh * HG_HEAD
    lmat, masks, nl = _hgrn_consts(c)

    def zspec(sec):
        return pl.BlockSpec((tblk, wblk), lambda s, h, b: (off + s * nblk + b, sec * hgrp + h))

    sspec = pl.BlockSpec((1, nh, HG_HEAD, HG_HEAD), lambda s, h, b: (s, h, 0, 0))
    body = functools.partial(_hgrn_body, c=c, nl=nl, nchunks=tblk // c, nh=nh)
    return _group_call(
        body, (nseq, hgrp, nblk),
        [zspec(0), zspec(1), zspec(2), zspec(3),
         pl.BlockSpec((1, wblk), lambda s, h, b: (0, h)),
         pl.BlockSpec((1, HG_HEAD), lambda s, h, b: (0, 0)),
         sspec,
         pl.BlockSpec(lmat.shape, lambda s, h, b: (0, 0)),
         pl.BlockSpec(masks.shape, lambda s, h, b: (0, 0, 0))],
        [z, z, z, z, lb.reshape(1, d), gnorm.reshape(1, HG_HEAD), s0, lmat, masks],
        [pl.BlockSpec((tblk, wblk), lambda s, h, b: (off + s * nblk + b, h)), sspec],
        [jax.ShapeDtypeStruct((m, d), bf16), jax.ShapeDtypeStruct((nseq, heads, HG_HEAD, HG_HEAD), f32)],
        dst, [pltpu.VMEM((nh, HG_HEAD, HG_HEAD), f32)], ("parallel", "parallel", "arbitrary"), "hgrn2_scan")


def _norm_shift_body(h_ref, halo_ref, g_ref, shp_ref, shs_ref, mu_ref, *o_refs, n_ptiles, tiles_per_seq, ts):
    i = pl.program_id(0)
    g = g_ref[...]

    def norm(v):
        return v * lax.rsqrt(jnp.mean(v * v, axis=-1, keepdims=True) + RMS_EPS) * g

    x = norm(h_ref[...])
    tm, d = x.shape
    row = lax.broadcasted_iota(jnp.int32, (tm, d), 0)
    rolled = pltpu.roll(x, 1, axis=0)

    def finish(xp):
        xx = xp - x
        for j, o_ref in enumerate(o_refs):
            o_ref[...] = (x + xx * mu_ref[j:j + 1, :]).astype(o_ref.dtype)

    @pl.when(i < n_ptiles)
    def _():
        first = jnp.where(i % tiles_per_seq == 0, shp_ref[0], norm(halo_ref[...])[7:8, :])
        finish(jnp.where(row == 0, first, rolled))

    @pl.when(i >= n_ptiles)
    def _():
        nsq = tm // ts
        first = jnp.broadcast_to(shs_ref[...], (nsq, ts, d)).reshape(tm, d)
        finish(jnp.where(row % ts == 0, first, rolled))


def _norm_shift_mix(h, g, mu, shift_p, shift_s, mp, tp, ts):
    m, d = h.shape
    ms = m - mp
    bp, bs = shift_p.shape[0], shift_s.shape[0]
    tm = next(t for t in (128, 64, 32, 16, 8) if tp % t == 0 and ms % t == 0 and t % ts == 0)
    n_ptiles = mp // tm
    tiles_per_seq = tp // tm
    nsq = tm // ts
    hb = tm // 8
    body = functools.partial(_norm_shift_body, n_ptiles=n_ptiles, tiles_per_seq=tiles_per_seq, ts=ts)
    return pl.pallas_call(
        body,
        grid=(m // tm,),
        in_specs=[pl.BlockSpec((tm, d), lambda i: (i, 0)),
                  pl.BlockSpec((8, d), lambda i: (jnp.maximum(i * hb - 1, 0), 0)),
                  pl.BlockSpec((1, d), lambda i: (0, 0)),
                  pl.BlockSpec((1, 1, d), lambda i: (jnp.minimum(i // tiles_per_seq, bp - 1), 0, 0)),
                  pl.BlockSpec((nsq, 1, d), lambda i: (jnp.maximum(i - n_ptiles, 0), 0, 0)),
                  pl.BlockSpec((6, d), lambda i: (0, 0))],
        out_specs=[pl.BlockSpec((tm, d), lambda i: (i, 0))] * 6,
        out_shape=[jax.ShapeDtypeStruct((m, d), bf16)] * 6,
        compiler_params=_params(("parallel",)),
        name="rwkv_norm_shift_mix",
    )(h, h, g.reshape(1, d), shift_p.reshape(bp, 1, d), shift_s.reshape(bs, 1, d), mu)


def _rwkv_body(r_ref, k_ref, v_ref, w_ref, a_ref, g_ref, kk_ref, ka_ref, rk_ref, lnw_ref, lnb_ref, s0_ref, dst_ref,
               o_ref, so_ref, st_ref, *, c, nchunks, npair):
    del dst_ref
    tb = pl.program_id(2)
    n = RW_HEAD
    w = 2 * n
    c2 = 2 * c
    i32 = jnp.int32

    @pl.when(tb == 0)
    def _():
        st_ref[...] = jnp.zeros(st_ref.shape, f32)
        for p in range(npair):
            st_ref[p, 0:n, 0:n] = s0_ref[0, 2 * p]
            st_ref[p, n:w, n:w] = s0_ref[0, 2 * p + 1]

    ri = lax.broadcasted_iota(i32, (c2, c2), 0)
    cj = lax.broadcasted_iota(i32, (c2, c2), 1)
    same = (ri // c) == (cj // c)
    low_bd = (same & (ri >= cj)).astype(f32)
    slow_bd = (same & (ri > cj)).astype(f32)
    eye = (ri == cj).astype(f32)
    lvl = []
    b = 1
    while b < c:
        lvl.append(((ri // (2 * b) == cj // (2 * b)) & (ri % (2 * b) >= b) & (cj % (2 * b) < b)).astype(f32))
        b *= 2
    lvl_b = [m.astype(bf16) for m in lvl]
    lowc = (lax.broadcasted_iota(i32, (c, c), 0) >= lax.broadcasted_iota(i32, (c, c), 1)).astype(bf16)
    m0 = lax.broadcasted_iota(i32, (c, w), 1) < n
    hmask = (lax.broadcasted_iota(i32, (c2, w), 0) // c) == (lax.broadcasted_iota(i32, (c2, w), 1) // n)

    lane0 = m0.astype(bf16)
    lane1 = 1.0 - lane0

    def stack(x_b):
        return jnp.concatenate([x_b * lane0, x_b * lane1], axis=0)

    def halfsums(x):
        s0_ = jnp.sum(jnp.where(m0, x, 0.0), axis=-1, keepdims=True)
        s1_ = jnp.sum(jnp.where(m0, 0.0, x), axis=-1, keepdims=True)
        return s0_, s1_

    def chunk(ci, carry):
        r0 = pl.multiple_of(ci * c, c)
        rows = pl.ds(r0, c)
        def pair_steps(p):
            cols = slice(p * w, (p + 1) * w)
            r = r_ref[rows, cols]
            k = k_ref[rows, cols]
            v = v_ref[rows, cols]
            lw = -jnp.exp(-jax.nn.softplus(-w_ref[rows, cols]) - 0.5)
            a = jax.nn.sigmoid(a_ref[rows, cols])
            kkp = k * kk_ref[:, cols]
            k2 = k * (1.0 + (a - 1.0) * ka_ref[:, cols])
            cum = _sum3(jnp.dot(lowc, _split3(lw), preferred_element_type=f32), w)
            yield
            cend = cum[c - 1:c, :]
            e_incl = jnp.exp(cum)
            e_inv = jnp.exp(-cum)
            e_end = jnp.exp(cend - cum)
            gam = jnp.exp(cend)
            n0, n1 = halfsums(kkp * kkp)
            rn0 = 1.0 / jnp.maximum(jnp.sqrt(n0), 1e-12)
            rn1 = 1.0 / jnp.maximum(jnp.sqrt(n1), 1e-12)
            kkn = kkp * jnp.where(m0, rn0, rn1)
            ka_ = kkn * a
            kkt_b = stack((kkn * jnp.exp(cum - lw)).astype(bf16))
            bh_b = stack((ka_ * e_inv).astype(bf16))
            bb_b = stack((ka_ * e_end).astype(bf16))
            rt_b = stack((r * e_incl).astype(bf16))
            kh_b = stack((k2 * e_inv).astype(bf16))
            kb_b = stack((k2 * e_end).astype(bf16))
            v_b = stack(v.astype(bf16))
            gram = _dnt(jnp.concatenate([kkt_b, rt_b], axis=0), jnp.concatenate([kh_b, bh_b], axis=0))
            yield
            mb = slow_bd * gram[0:c2, c2:2 * c2]
            mk_b = (slow_bd * gram[0:c2, 0:c2]).astype(bf16)
            pk_b = (low_bd * gram[c2:2 * c2, 0:c2]).astype(bf16)
            pb_b = (low_bd * gram[c2:2 * c2, c2:2 * c2]).astype(bf16)
            mkv = _d(mk_b, v_b)
            pkv = _d(pk_b, v_b)
            vk = _dtn(v_b, kb_b)
            tm = eye - lvl[0] * mb
            mb_b = mb.astype(bf16)
            for l in range(1, len(lvl)):
                tm_b = tm.astype(bf16)
                tc = _d(tm_b, mb_b * lvl_b[l])
                yield
                tm = tm - _d(tc.astype(bf16), tm_b)
                yield
            wu = _d(tm.astype(bf16), jnp.concatenate([kkt_b, mkv.astype(bf16)], axis=1))
            yield
            s = st_ref[p]
            s_b = s.astype(bf16)
            u_b = (_dnt(wu[:, 0:w].astype(bf16), s_b) + wu[:, w:2 * w]).astype(bf16)
            yr = _dnt(rt_b, s_b) + pkv
            yield
            y_s = yr - _d(pb_b, u_b)
            st_ref[p] = s * gam + vk - _dtn(u_b, bb_b)
            yield
            mean = jnp.sum(y_s, axis=-1, keepdims=True) * (1.0 / n)
            yc = jnp.where(hmask, y_s - mean, 0.0)
            var = jnp.sum(yc * yc, axis=-1, keepdims=True) * (1.0 / n)
            yn_s = yc * lax.rsqrt(var + GN_EPS)
            yn = (yn_s[0:c] + yn_s[c:c2]) * lnw_ref[:, cols] + lnb_ref[:, cols]
            b0, b1 = halfsums(r * k2 * rk_ref[:, cols])
            bonus = jnp.where(m0, b0, b1) * v
            o_ref[rows, cols] = ((yn + bonus) * g_ref[rows, cols]).astype(o_ref.dtype)
            yield

        _round_robin([pair_steps(p) for p in range(npair)])
        return carry

    lax.fori_loop(0, nchunks, chunk, 0)

    @pl.when(tb == pl.num_programs(2) - 1)
    def _():
        for p in range(npair):
            so_ref[0, 2 * p] = st_ref[p, 0:n, 0:n]
            so_ref[0, 2 * p + 1] = st_ref[p, n:w, n:w]


def _rwkv(acts, vecs, s0, row_off, nseq, t, dst):
    m, d = acts[0].shape
    heads = d // RW_HEAD
    pairs = heads // 2
    npair = _pick(pairs, (RW_PAIRS_PER_STEP, 4, 2, 1))
    c = min(RW_CHUNK, t)
    tblk = _pick(t, (256, 128, 64, 32))
    nblk = t // tblk
    off = row_off // tblk
    wblk = npair * 2 * RW_HEAD
    aspec = pl.BlockSpec((tblk, wblk), lambda s, h, b: (off + s * nblk + b, h))
    vspec = pl.BlockSpec((1, wblk), lambda s, h, b: (0, h))
    sspec = pl.BlockSpec((1, 2 * npair, RW_HEAD, RW_HEAD), lambda s, h, b: (s, h, 0, 0))
    body = functools.partial(_rwkv_body, c=c, nchunks=tblk // c, npair=npair)
    return _group_call(
        body, (nseq, pairs // npair, nblk),
        [aspec] * 6 + [vspec] * 5 + [sspec],
        list(acts) + [x.reshape(1, d) for x in vecs] + [s0],
        [aspec, sspec],
        [jax.ShapeDtypeStruct((m, d), bf16), jax.ShapeDtypeStruct((nseq, heads, RW_HEAD, RW_HEAD), f32)],
        dst, [pltpu.VMEM((npair, 2 * RW_HEAD, 2 * RW_HEAD), f32)], ("parallel", "parallel", "arbitrary"),
        "rwkv7_scan")


def _upconv_body(x_ref, wg_ref, wu_ref, c0p_ref, c0s_ref, cw_ref, cb_ref, o_ref, carry_ref, *,
                 n_ptiles, tiles_per_seq, ts, nsplit):
    i = pl.program_id(0)
    j = pl.program_id(1)
    tm = x_ref.shape[0]
    tf = o_ref.shape[1]
    th = tm // nsplit
    cw = cw_ref[...]
    row = lax.broadcasted_iota(jnp.int32, (th, tf), 0)

    def part(hidx, t, hm1, hm2):
        rows = slice(hidx * th, (hidx + 1) * th)
        x = x_ref[rows, :]
        hg = jnp.dot(x, wg_ref[...], preferred_element_type=f32)
        hu = jnp.dot(x, wu_ref[...], preferred_element_type=f32)
        p1 = jnp.where(t == 0, hm1, pltpu.roll(hg, 1, axis=0))
        p2 = jnp.where(t == 0, hm2, jnp.where(t == 1, hm1, pltpu.roll(hg, 2, axis=0)))
        hc = cb_ref[...] + (cw[0:1, :] * p2 + cw[1:2, :] * p1 + cw[2:3, :] * hg)
        act = 0.5 * hc * (1.0 + lax.erf(hc * 0.7071067811865476))
        o_ref[rows, :] = (act * hu).astype(o_ref.dtype)
        return hg[th - 2:th, :]

    @pl.when(i < n_ptiles)
    def _():
        @pl.when(i % tiles_per_seq == 0)
        def _():
            carry_ref[j] = c0p_ref[0]
        prev = carry_ref[j]
        for hidx in range(nsplit):
            prev = part(hidx, row, prev[1:2, :], prev[0:1, :])
        carry_ref[j] = prev

    @pl.when(i >= n_ptiles)
    def _():
        nsq = th // ts
        for hidx in range(nsplit):
            c0 = c0s_ref[hidx * nsq:(hidx + 1) * nsq]
            hm1 = jnp.broadcast_to(c0[:, 1:2, :], (nsq, ts, tf)).reshape(th, tf)
            hm2 = jnp.broadcast_to(c0[:, 0:1, :], (nsq, ts, tf)).reshape(th, tf)
            part(hidx, row % ts, hm1, hm2)


def _ffn_up_conv(xn, w_up, c0p, c0s, cw, cb, mp, tp, ts):
    m, k = xn.shape
    f = w_up.shape[1] // 2
    ms = m - mp
    tm = next(t for t in (1024, 512, 256, 128, 64, 32, 16, 8) if tp % t == 0 and ms % t == 0 and t % ts == 0)
    tf = _pick(f, (256, 128))
    nf = f // tf
    n_ptiles = mp // tm
    tiles_per_seq = tp // tm
    nsq = tm // ts
    bp = c0p.shape[0]
    body = functools.partial(_upconv_body, n_ptiles=n_ptiles, tiles_per_seq=tiles_per_seq, ts=ts, nsplit=1)
    return pl.pallas_call(
        body,
        grid=(m // tm, nf),
        in_specs=[pl.BlockSpec((tm, k), lambda i, j: (i, 0)),
                  pl.BlockSpec((k, tf), lambda i, j: (0, j)),
                  pl.BlockSpec((k, tf), lambda i, j: (0, nf + j)),
                  pl.BlockSpec((1, 2, tf), lambda i, j: (jnp.minimum(i // tiles_per_seq, bp - 1), 0, j)),
                  pl.BlockSpec((nsq, 2, tf), lambda i, j: (jnp.maximum(i - n_ptiles, 0), 0, j)),
                  pl.BlockSpec((3, tf), lambda i, j: (0, j)),
                  pl.BlockSpec((1, tf), lambda i, j: (0, j))],
        out_specs=pl.BlockSpec((tm, tf), lambda i, j: (i, j)),
        out_shape=jax.ShapeDtypeStruct((m, f), bf16),
        scratch_shapes=[pltpu.VMEM((nf, 2, tf), f32)],
        compiler_params=_params(("arbitrary", "arbitrary")),
        name="ffn_up_conv",
    )(xn, w_up, w_up, c0p, c0s, cw, cb.reshape(1, f))


def _first(accs, tiles, rows):
    return accs[0]


def _resid(accs, tiles, rows):
    return tiles[0] + accs[0]


def _bias(accs, tiles, rows):
    return accs[0] + rows[0]


def _ple(accs, tiles, rows):
    return tiles[0] + jax.nn.sigmoid(accs[0]) * accs[1]


def _tanh(accs, tiles, rows):
    return jnp.tanh(accs[0])


def _sigm(accs, tiles, rows):
    return jax.nn.sigmoid(accs[0])


def kernel(x_prompt, x_sample, p_prompt, p_sample, state_hgrn, state_rwkv, state_shift, state_ffn_conv, norm_mix, norm_ffn, norm_ple, norm_final, hg_w_in, hg_lb_logits, hg_gnorm, hg_w_o, rw_mu, rw_w_rkv, rw_w0, rw_w1, rw_w2, rw_a0, rw_a1, rw_a2, rw_g1, rw_g2, rw_k_k, rw_k_a, rw_r_k, rw_lnx_w, rw_lnx_b, rw_w_o, ffn_w_up, ffn_conv_w, ffn_conv_b, ffn_w_down, ple_w_proj, ple_w_gate):
    bp, tp, d = x_prompt.shape
    bs, ts, _ = x_sample.shape
    depth = norm_mix.shape[0]
    mp, ms = bp * tp, bs * ts
    groups = ((0, bp, tp), (mp, bs, ts))
    f_ff = ffn_conv_b.shape[1]
    cast = lambda w: w.astype(bf16)

    h = jnp.concatenate([x_prompt.reshape(mp, d), x_sample.reshape(ms, d)], axis=0)
    p_all = jnp.concatenate([p_prompt.reshape(depth, mp, -1), p_sample.reshape(depth, ms, -1)], axis=1).astype(bf16)
    lb_all = jnp.cumsum(jax.nn.softmax(hg_lb_logits.astype(f32), axis=0), axis=0)

    hg_states = [[], []]
    rw_states = [[], []]
    sh_states = [[], []]
    cv_states = [[], []]
    for i in range(depth):
        j = i // 2
        if i % 2 == 0:
            xn = _rmsnorm(h, norm_mix[i], bf16)
            z = _mm([(xn, cast(hg_w_in[j]))], _first, f32, name="hgrn_in")
            mix_in = jnp.zeros((mp + ms, d), bf16)
            for gi, (off, nseq, t) in enumerate(groups):
                s0 = jnp.zeros((nseq, d // HG_HEAD, HG_HEAD, HG_HEAD), f32) if gi == 0 else state_hgrn[j].astype(f32)
                mix_in, s_new = _hgrn(z, lb_all[j], hg_gnorm[j], s0, off, nseq, t, mix_in)
                hg_states[gi].append(s_new)
            h = _mm([(mix_in, cast(hg_w_o[j]))], _resid, f32, tiles=(h,), name="hgrn_out")
        else:
            xr, xw, xk, xv, xa, xg = _norm_shift_mix(h, norm_mix[i], rw_mu[j], jnp.zeros((bp, d), f32),
                                                     state_shift[j].astype(f32), mp, tp, ts)
            h_last = jnp.concatenate([h[tp - 1:mp:tp], h[mp + ts - 1::ts]], axis=0)
            xn_last = _rmsnorm(h_last, norm_mix[i], f32)
            sh_states[0].append(xn_last[:bp])
            sh_states[1].append(xn_last[bp:])
            w_rkv = cast(rw_w_rkv[j])
            r = _mm([(xr, w_rkv)], _first, f32, ncols=d, name="rwkv_r")
            k = _mm([(xk, w_rkv)], _first, f32, ncols=d, col_off=d, name="rwkv_k")
            v = _mm([(xv, w_rkv)], _first, f32, ncols=d, col_off=2 * d, name="rwkv_v")
            w_lin = _mm([(_mm([(xw, cast(rw_w1[j]))], _tanh, bf16, name="rwkv_w1"), cast(rw_w2[j]))],
                        _bias, f32, rows=(rw_w0[j],), name="rwkv_w2")
            a_lin = _mm([(_mm([(xa, cast(rw_a1[j]))], _first, bf16, name="rwkv_a1"), cast(rw_a2[j]))],
                        _bias, f32, rows=(rw_a0[j],), name="rwkv_a2")
            g = _mm([(_mm([(xg, cast(rw_g1[j]))], _sigm, bf16, name="rwkv_g1"), cast(rw_g2[j]))],
                    _first, f32, name="rwkv_g2")
            vecs = (rw_k_k[j], rw_k_a[j], rw_r_k[j].reshape(d), rw_lnx_w[j], rw_lnx_b[j])
            mix_in = jnp.zeros((mp + ms, d), bf16)
            for gi, (off, nseq, t) in enumerate(groups):
                s0 = jnp.zeros((nseq, d // RW_HEAD, RW_HEAD, RW_HEAD), f32) if gi == 0 else state_rwkv[j].astype(f32)
                mix_in, s_new = _rwkv((r, k, v, w_lin, a_lin, g), vecs, s0, off, nseq, t, mix_in)
                rw_states[gi].append(s_new)
            h = _mm([(mix_in, cast(rw_w_o[j]))], _resid, f32, tiles=(h,), name="rwkv_out")

        xn = _rmsnorm(h, norm_ffn[i], bf16)
        w_up = cast(ffn_w_up[i])
        act = _ffn_up_conv(xn, w_up, jnp.zeros((bp, 2, f_ff), f32), state_ffn_conv[i].astype(f32),
                           ffn_conv_w[i], ffn_conv_b[i], mp, tp, ts)
        x_last = jnp.concatenate([xn[:mp].reshape(bp, tp, d)[:, tp - 2:].reshape(2 * bp, d),
                                  xn[mp:].reshape(bs, ts, d)[:, ts - 2:].reshape(2 * bs, d)], axis=0)
        c_new = _mm([(x_last, w_up)], _first, f32, ncols=f_ff, name="ffn_conv_state")
        cv_states[0].append(c_new[:2 * bp].reshape(bp, 2, f_ff))
        cv_states[1].append(c_new[2 * bp:].reshape(bs, 2, f_ff))
        h = _mm([(act, cast(ffn_w_down[i]))], _resid, f32, tiles=(h,), tm=512, tn=256, name="ffn_down")

        xn = _rmsnorm(h, norm_ple[i], bf16)
        h = _mm([(xn, cast(ple_w_gate[i])), (p_all[i], cast(ple_w_proj[i]))], _ple, f32, tiles=(h,), tn=512,
                name="ple")

    y_prompt = _rmsnorm(h, norm_final, f32, 0, mp).reshape(bp, tp, d)
    y_sample = _rmsnorm(h, norm_final, f32, mp, ms).reshape(bs, ts, d)
    st = lambda xs: jnp.stack(xs)
    return (y_prompt, y_sample,
            st(hg_states[0]), st(rw_states[0]), st(sh_states[0]), st(cv_states[0]),
            st(hg_states[1]), st(rw_states[1]), st(sh_states[1]), st(cv_states[1]))
```

```python
import functools

import numpy as np
import jax
import jax.numpy as jnp
from jax import lax
from jax.experimental import pallas as pl
from jax.experimental.pallas import tpu as pltpu

f32 = jnp.float32
bf16 = jnp.bfloat16

RMS_EPS = 1e-6
GN_EPS = 64e-5
HG_HEAD = 128
HG_CHUNK = 128
HG_HEADS_PER_STEP = 4
RW_HEAD = 64
RW_CHUNK = 64
RW_PAIRS_PER_STEP = 8
VMEM_LIMIT = 56 * 1024 * 1024


def _pick(n, prefs):
    for p in prefs:
        if n % p == 0:
            return p
    return n


def _params(sem):
    return pltpu.CompilerParams(dimension_semantics=sem, vmem_limit_bytes=VMEM_LIMIT)


def _nn(a, b):
    return jnp.dot(a.astype(bf16), b.astype(bf16), preferred_element_type=f32)


def _nt(a, b):
    return lax.dot_general(a.astype(bf16), b.astype(bf16), (((1,), (1,)), ((), ())), preferred_element_type=f32)


def _tn(a, b):
    return lax.dot_general(a.astype(bf16), b.astype(bf16), (((0,), (0,)), ((), ())), preferred_element_type=f32)


def _d(a, b):
    return jnp.dot(a, b, preferred_element_type=f32)


def _dnt(a, b):
    return lax.dot_general(a, b, (((1,), (1,)), ((), ())), preferred_element_type=f32)


def _dtn(a, b):
    return lax.dot_general(a, b, (((0,), (0,)), ((), ())), preferred_element_type=f32)


def _split3(x):
    hi = x.astype(bf16)
    r1 = x - hi.astype(f32)
    mid = r1.astype(bf16)
    lo = (r1 - mid.astype(f32)).astype(bf16)
    return jnp.concatenate([hi, mid, lo], axis=1)


def _sum3(d3, w):
    return d3[:, 0:w] + (d3[:, w:2 * w] + d3[:, 2 * w:3 * w])


def _round_robin(gens):
    for _ in zip(*gens):
        pass


def _group_call(body, grid, in_specs, ins, out_specs, out_shapes, dst, scratch, sem, name):
    return pl.pallas_call(
        body,
        grid=grid, in_specs=list(in_specs) + [pl.BlockSpec(memory_space=pl.ANY)],
        out_specs=out_specs, out_shape=out_shapes,
        scratch_shapes=scratch, input_output_aliases={len(ins): 0},
        compiler_params=_params(sem), name=name,
    )(*ins, dst)


def _rms_body(x_ref, g_ref, o_ref):
    x = x_ref[...]
    ms = jnp.mean(x * x, axis=-1, keepdims=True)
    o_ref[...] = (x * lax.rsqrt(ms + RMS_EPS) * g_ref[...]).astype(o_ref.dtype)


def _rmsnorm(x, g, out_dtype, row_off=0, nrows=None):
    d = x.shape[1]
    m = x.shape[0] if nrows is None else nrows
    tm = _pick(m, (256, 128, 64, 32, 16, 8))
    off = row_off // tm
    return pl.pallas_call(
        _rms_body,
        grid=(m // tm,),
        in_specs=[pl.BlockSpec((tm, d), lambda i: (off + i, 0)), pl.BlockSpec((1, d), lambda i: (0, 0))],
        out_specs=pl.BlockSpec((tm, d), lambda i: (i, 0)),
        out_shape=jax.ShapeDtypeStruct((m, d), out_dtype),
        compiler_params=_params(("parallel",)),
        name="rmsnorm",
    )(x, g.reshape(1, d))


def _mm_body(*refs, nd, nt, nr, epi, stationary):
    dots = refs[: 2 * nd]
    tiles = refs[2 * nd: 2 * nd + nt]
    rows = refs[2 * nd + nt: 2 * nd + nt + nr]
    o_ref = refs[2 * nd + nt + nr]
    if stationary:
        wbs = refs[2 * nd + nt + nr + 1:]

        @pl.when(pl.program_id(1) == 0)
        def _():
            for q in range(nd):
                wbs[q][...] = dots[2 * q + 1][...].astype(bf16)
        ws = [wb[...] for wb in wbs]
    else:
        ws = [dots[2 * q + 1][...] for q in range(nd)]
    accs = [jnp.dot(dots[2 * q][...], ws[q], preferred_element_type=f32) for q in range(nd)]
    o_ref[...] = epi(accs, [t[...] for t in tiles], [r[...] for r in rows]).astype(o_ref.dtype)


def _mm(dots, epi, out_dtype, tiles=(), rows=(), tm=1024, tn=512, ncols=None, col_off=0, stationary=True, name="mm"):
    m = dots[0][0].shape[0]
    n = dots[0][1].shape[2] if ncols is None else ncols
    tm = _pick(m, (tm, 512, 256, 128, 64, 32, 16, 8))
    tn = _pick(n, (tn, 512, 256, 128))
    assert col_off % tn == 0
    joff = col_off // tn
    if stationary:
        grid = (n // tn, m // tm)
        ij = lambda a, b: (b, a)
    else:
        grid = (m // tm, n // tn)
        ij = lambda a, b: (a, b)
    ins, specs, scratch = [], [], []
    for x, w, layer in dots:
        k = x.shape[1]
        ins += [x, w]
        specs += [pl.BlockSpec((tm, k), lambda a, b: (ij(a, b)[0], 0)),
                  pl.BlockSpec((None, k, tn), lambda a, b, _l=layer: (_l, 0, joff + ij(a, b)[1]))]
        if stationary:
            scratch.append(pltpu.VMEM((k, tn), bf16))
    for t in tiles:
        ins.append(t)
        specs.append(pl.BlockSpec((tm, tn), lambda a, b: ij(a, b)))
    for r in rows:
        ins.append(r.reshape(1, n))
        specs.append(pl.BlockSpec((1, tn), lambda a, b: (0, ij(a, b)[1])))
    body = functools.partial(_mm_body, nd=len(dots), nt=len(tiles), nr=len(rows), epi=epi, stationary=stationary)
    return pl.pallas_call(
        body,
        grid=grid,
        in_specs=specs,
        out_specs=pl.BlockSpec((tm, tn), lambda a, b: ij(a, b)),
        out_shape=jax.ShapeDtypeStruct((m, n), out_dtype),
        scratch_shapes=scratch,
        compiler_params=_params(("parallel", "arbitrary")),
        name=name,
    )(*ins)


def _hgrn_consts(c):
    idx = np.arange(c)
    mats = [(idx[:, None] >= idx[None, :])]
    masks = [np.eye(c, dtype=bool)]
    b = 1
    while b < c:
        blk = idx // (2 * b)
        second = (idx % (2 * b)) >= b
        bnd = blk * 2 * b + b - 1
        lq = (idx[None, :] > bnd[:, None]) & (idx[None, :] <= idx[:, None]) & second[:, None]
        lk = (idx[None, :] > idx[:, None]) & (idx[None, :] <= bnd[:, None]) & (~second)[:, None]
        mats.append(lq | lk)
        masks.append((blk[:, None] == blk[None, :]) & second[:, None] & (~second)[None, :])
        b *= 2
    lmat = np.concatenate(mats, axis=0).astype(np.float32)
    return jnp.asarray(lmat, dtype=bf16), jnp.asarray(np.stack(masks).astype(np.float32)), len(mats) - 1


def _hgrn_body(q_ref, f_ref, i_ref, g_ref, lb_ref, gn_ref, s0_ref, l_ref, mask_ref, dst_ref, o_ref, so_ref, st_ref, *,
               c, nl, nchunks, nh):
    del dst_ref
    tb = pl.program_id(2)
    w = HG_HEAD

    @pl.when(tb == 0)
    def _():
        for hh in range(nh):
            st_ref[hh] = s0_ref[0, hh].T

    gn = gn_ref[...]

    def chunk(ci, carry):
        r0 = pl.multiple_of(ci * c, c)
        rows = pl.ds(r0, c)
        def head_steps(hh):
            cols = slice(hh * w, (hh + 1) * w)
            lb = lb_ref[:, cols]
            q = q_ref[rows, cols]
            fg = lb + (1.0 - lb) * jax.nn.sigmoid(f_ref[rows, cols])
            gl = jnp.log(fg)
            kk = 1.0 - fg
            qs = q * jax.nn.sigmoid(q)
            v = i_ref[rows, cols]
            d = _sum3(jnp.dot(l_ref[...], _split3(gl), preferred_element_type=f32), w)
            a = mask_ref[0] * _nt(qs, kk)
            yield
            gcum = d[0:c]
            gend = gcum[c - 1:c, :]
            st = st_ref[hh]
            oi = _nt(qs * jnp.exp(gcum), st)
            st_ref[hh] = st * jnp.exp(gend) + _tn(v, kk * jnp.exp(gend - gcum))
            for l in range(nl):
                e = jnp.exp(d[(1 + l) * c:(2 + l) * c])
                a = a + mask_ref[1 + l] * _nt(qs * e, kk * e)
            yield
            o = _nn(a, v) + oi
            yield
            ms = jnp.mean(o * o, axis=-1, keepdims=True)
            on = o * lax.rsqrt(ms + RMS_EPS) * gn
            o_ref[rows, cols] = (on * jax.nn.sigmoid(g_ref[rows, cols])).astype(o_ref.dtype)
            yield

        _round_robin([head_steps(hh) for hh in range(nh)])
        return carry

    lax.fori_loop(0, nchunks, chunk, 0)

    @pl.when(tb == pl.num_programs(2) - 1)
    def _():
        for hh in range(nh):
            so_ref[0, hh] = st_ref[hh].T


def _hgrn(z, lb, gnorm, s0, row_off, nseq, t, dst):
    m = z.shape[0]
    d = z.shape[1] // 4
    heads = d // HG_HEAD
    c = min(HG_CHUNK, t)
    nh = _pick(heads, (HG_HEADS_PER_STEP, 2, 1))
    tblk = _pick(t, (512, 256, 128, 64, 32))
    nblk = t // tblk
    off = row_off // tblk
    hgrp = heads // nh
    wblk = nh * HG_HEAD
    lmat, masks, nl = _hgrn_consts(c)

    def zspec(sec):
        return pl.BlockSpec((tblk, wblk), lambda s, h, b: (off + s * nblk + b, sec * hgrp + h))

    sspec = pl.BlockSpec((1, nh, HG_HEAD, HG_HEAD), lambda s, h, b: (s, h, 0, 0))
    body = functools.partial(_hgrn_body, c=c, nl=nl, nchunks=tblk // c, nh=nh)
    return _group_call(
        body, (nseq, hgrp, nblk),
        [zspec(0), zspec(1), zspec(2), zspec(3),
         pl.BlockSpec((1, wblk), lambda s, h, b: (0, h)),
         pl.BlockSpec((1, HG_HEAD), lambda s, h, b: (0, 0)),
         sspec,
         pl.BlockSpec(lmat.shape, lambda s, h, b: (0, 0)),
         pl.BlockSpec(masks.shape, lambda s, h, b: (0, 0, 0))],
        [z, z, z, z, lb.reshape(1, d), gnorm.reshape(1, HG_HEAD), s0, lmat, masks],
        [pl.BlockSpec((tblk, wblk), lambda s, h, b: (off + s * nblk + b, h)), sspec],
        [jax.ShapeDtypeStruct((m, d), bf16), jax.ShapeDtypeStruct((nseq, heads, HG_HEAD, HG_HEAD), f32)],
        dst, [pltpu.VMEM((nh, HG_HEAD, HG_HEAD), f32)], ("parallel", "parallel", "arbitrary"), "hgrn2_scan")


def _norm_shift_body(h_ref, halo_ref, g_ref, shp_ref, shs_ref, mu_ref, *o_refs, n_ptiles, tiles_per_seq, ts):
    i = pl.program_id(0)
    g = g_ref[...]

    def norm(v):
        return v * lax.rsqrt(jnp.mean(v * v, axis=-1, keepdims=True) + RMS_EPS) * g

    x = norm(h_ref[...])
    tm, d = x.shape
    row = lax.broadcasted_iota(jnp.int32, (tm, d), 0)
    rolled = pltpu.roll(x, 1, axis=0)

    def finish(xp):
        xx = xp - x
        for j, o_ref in enumerate(o_refs):
            o_ref[...] = (x + xx * mu_ref[j:j + 1, :]).astype(o_ref.dtype)

    @pl.when(i < n_ptiles)
    def _():
        first = jnp.where(i % tiles_per_seq == 0, shp_ref[0], norm(halo_ref[...])[7:8, :])
        finish(jnp.where(row == 0, first, rolled))

    @pl.when(i >= n_ptiles)
    def _():
        nsq = tm // ts
        first = jnp.broadcast_to(shs_ref[...], (nsq, ts, d)).reshape(tm, d)
        finish(jnp.where(row % ts == 0, first, rolled))


def _norm_shift_mix(h, g, mu, shift_p, shift_s, mp, tp, ts):
    m, d = h.shape
    ms = m - mp
    bp, bs = shift_p.shape[0], shift_s.shape[0]
    tm = next(t for t in (128, 64, 32, 16, 8) if tp % t == 0 and ms % t == 0 and t % ts == 0)
    n_ptiles = mp // tm
    tiles_per_seq = tp // tm
    nsq = tm // ts
    hb = tm // 8
    body = functools.partial(_norm_shift_body, n_ptiles=n_ptiles, tiles_per_seq=tiles_per_seq, ts=ts)
    return pl.pallas_call(
        body,
        grid=(m // tm,),
        in_specs=[pl.BlockSpec((tm, d), lambda i: (i, 0)),
                  pl.BlockSpec((8, d), lambda i: (jnp.maximum(i * hb - 1, 0), 0)),
                  pl.BlockSpec((1, d), lambda i: (0, 0)),
                  pl.BlockSpec((1, 1, d), lambda i: (jnp.minimum(i // tiles_per_seq, bp - 1), 0, 0)),
                  pl.BlockSpec((nsq, 1, d), lambda i: (jnp.maximum(i - n_ptiles, 0), 0, 0)),
                  pl.BlockSpec((6, d), lambda i: (0, 0))],
        out_specs=[pl.BlockSpec((tm, d), lambda i: (i, 0))] * 6,
        out_shape=[jax.ShapeDtypeStruct((m, d), bf16)] * 6,
        compiler_params=_params(("parallel",)),
        name="rwkv_norm_shift_mix",
    )(h, h, g.reshape(1, d), shift_p.reshape(bp, 1, d), shift_s.reshape(bs, 1, d), mu)


def _rwkv_body(r_ref, k_ref, v_ref, w_ref, a_ref, g_ref, kk_ref, ka_ref, rk_ref, lnw_ref, lnb_ref, s0_ref, dst_ref,
               o_ref, so_ref, st_ref, *, c, nchunks, npair):
    del dst_ref
    tb = pl.program_id(2)
    n = RW_HEAD
    w = 2 * n
    c2 = 2 * c
    i32 = jnp.int32

    @pl.when(tb == 0)
    def _():
        st_ref[...] = jnp.zeros(st_ref.shape, f32)
        for p in range(npair):
            st_ref[p, 0:n, 0:n] = s0_ref[0, 2 * p]
            st_ref[p, n:w, n:w] = s0_ref[0, 2 * p + 1]

    ri = lax.broadcasted_iota(i32, (c2, c2), 0)
    cj = lax.broadcasted_iota(i32, (c2, c2), 1)
    same = (ri // c) == (cj // c)
    low_bd = (same & (ri >= cj)).astype(f32)
    slow_bd = (same & (ri > cj)).astype(f32)
    eye = (ri == cj).astype(f32)
    lvl = []
    b = 1
    while b < c:
        lvl.append(((ri // (2 * b) == cj // (2 * b)) & (ri % (2 * b) >= b) & (cj % (2 * b) < b)).astype(f32))
        b *= 2
    lvl_b = [m.astype(bf16) for m in lvl]
    lowc = (lax.broadcasted_iota(i32, (c, c), 0) >= lax.broadcasted_iota(i32, (c, c), 1)).astype(bf16)
    m0 = lax.broadcasted_iota(i32, (c, w), 1) < n
    hmask = (lax.broadcasted_iota(i32, (c2, w), 0) // c) == (lax.broadcasted_iota(i32, (c2, w), 1) // n)

    lane0 = m0.astype(bf16)
    lane1 = 1.0 - lane0

    def stack(x_b):
        return jnp.concatenate([x_b * lane0, x_b * lane1], axis=0)

    def halfsums(x):
        s0_ = jnp.sum(jnp.where(m0, x, 0.0), axis=-1, keepdims=True)
        s1_ = jnp.sum(jnp.where(m0, 0.0, x), axis=-1, keepdims=True)
        return s0_, s1_

    def chunk(ci, carry):
        r0 = pl.multiple_of(ci * c, c)
        rows = pl.ds(r0, c)
        def pair_steps(p):
            cols = slice(p * w, (p + 1) * w)
            r = r_ref[rows, cols]
            k = k_ref[rows, cols]
            v = v_ref[rows, cols]
            lw = -jnp.exp(-jax.nn.softplus(-w_ref[rows, cols]) - 0.5)
            a = jax.nn.sigmoid(a_ref[rows, cols])
            kkp = k * kk_ref[:, cols]
            k2 = k * (1.0 + (a - 1.0) * ka_ref[:, cols])
            cum = _sum3(jnp.dot(lowc, _split3(lw), preferred_element_type=f32), w)
            yield
            cend = cum[c - 1:c, :]
            e_incl = jnp.exp(cum)
            e_inv = jnp.exp(-cum)
            e_end = jnp.exp(cend - cum)
            gam = jnp.exp(cend)
            n0, n1 = halfsums(kkp * kkp)
            rn0 = 1.0 / jnp.maximum(jnp.sqrt(n0), 1e-12)
            rn1 = 1.0 / jnp.maximum(jnp.sqrt(n1), 1e-12)
            kkn = kkp * jnp.where(m0, rn0, rn1)
            ka_ = kkn * a
            kkt_b = stack((kkn * jnp.exp(cum - lw)).astype(bf16))
            bh_b = stack((ka_ * e_inv).astype(bf16))
            bb_b = stack((ka_ * e_end).astype(bf16))
            rt_b = stack((r * e_incl).astype(bf16))
            kh_b = stack((k2 * e_inv).astype(bf16))
            kb_b = stack((k2 * e_end).astype(bf16))
            v_b = stack(v.astype(bf16))
            gram = _dnt(jnp.concatenate([kkt_b, rt_b], axis=0), jnp.concatenate([kh_b, bh_b], axis=0))
            yield
            mb = slow_bd * gram[0:c2, c2:2 * c2]
            mk_b = (slow_bd * gram[0:c2, 0:c2]).astype(bf16)
            pk_b = (low_bd * gram[c2:2 * c2, 0:c2]).astype(bf16)
            pb_b = (low_bd * gram[c2:2 * c2, c2:2 * c2]).astype(bf16)
            mkv = _d(mk_b, v_b)
            pkv = _d(pk_b, v_b)
            vk = _dtn(v_b, kb_b)
            tm = eye - lvl[0] * mb
            mb_b = mb.astype(bf16)
            for l in range(1, len(lvl)):
                tm_b = tm.astype(bf16)
                tc = _d(tm_b, mb_b * lvl_b[l])
                yield
                tm = tm - _d(tc.astype(bf16), tm_b)
                yield
            wu = _d(tm.astype(bf16), jnp.concatenate([kkt_b, mkv.astype(bf16)], axis=1))
            yield
            s = st_ref[p]
            s_b = s.astype(bf16)
            u_b = (_dnt(wu[:, 0:w].astype(bf16), s_b) + wu[:, w:2 * w]).astype(bf16)
            yr = _dnt(rt_b, s_b) + pkv
            yield
            y_s = yr - _d(pb_b, u_b)
            st_ref[p] = s * gam + vk - _dtn(u_b, bb_b)
            yield
            mean = jnp.sum(y_s, axis=-1, keepdims=True) * (1.0 / n)
            yc = jnp.where(hmask, y_s - mean, 0.0)
            var = jnp.sum(yc * yc, axis=-1, keepdims=True) * (1.0 / n)
            yn_s = yc * lax.rsqrt(var + GN_EPS)
            yn = (yn_s[0:c] + yn_s[c:c2]) * lnw_ref[:, cols] + lnb_ref[:, cols]
            b0, b1 = halfsums(r * k2 * rk_ref[:, cols])
            bonus = jnp.where(m0, b0, b1) * v
            o_ref[rows, cols] = ((yn + bonus) * g_ref[rows, cols]).astype(o_ref.dtype)
            yield

        _round_robin([pair_steps(p) for p in range(npair)])
        return carry

    lax.fori_loop(0, nchunks, chunk, 0)

    @pl.when(tb == pl.num_programs(2) - 1)
    def _():
        for p in range(npair):
            so_ref[0, 2 * p] = st_ref[p, 0:n, 0:n]
            so_ref[0, 2 * p + 1] = st_ref[p, n:w, n:w]


def _rwkv(acts, vecs, s0, row_off, nseq, t, dst):
    m, d = acts[0].shape
    heads = d // RW_HEAD
    pairs = heads // 2
    npair = _pick(pairs, (RW_PAIRS_PER_STEP, 4, 2, 1))
    c = min(RW_CHUNK, t)
    tblk = _pick(t, (256, 128, 64, 32))
    nblk = t // tblk
    off = row_off // tblk
    wblk = npair * 2 * RW_HEAD
    aspec = pl.BlockSpec((tblk, wblk), lambda s, h, b: (off + s * nblk + b, h))
    vspec = pl.BlockSpec((1, wblk), lambda s, h, b: (0, h))
    sspec = pl.BlockSpec((1, 2 * npair, RW_HEAD, RW_HEAD), lambda s, h, b: (s, h, 0, 0))
    body = functools.partial(_rwkv_body, c=c, nchunks=tblk // c, npair=npair)
    return _group_call(
        body, (nseq, pairs // npair, nblk),
        [aspec] * 6 + [vspec] * 5 + [sspec],
        list(acts) + [x.reshape(1, d) for x in vecs] + [s0],
        [aspec, sspec],
        [jax.ShapeDtypeStruct((m, d), bf16), jax.ShapeDtypeStruct((nseq, heads, RW_HEAD, RW_HEAD), f32)],
        dst, [pltpu.VMEM((npair, 2 * RW_HEAD, 2 * RW_HEAD), f32)], ("parallel", "parallel", "arbitrary"),
        "rwkv7_scan")


def _upconv_body(x_ref, wg_ref, wu_ref, c0p_ref, c0s_ref, cw_ref, cb_ref, o_ref, wgb_ref, wub_ref, carry_ref, *,
                 n_ptiles, tiles_per_seq, ts):
    i = pl.program_id(1)

    @pl.when(i == 0)
    def _():
        wgb_ref[...] = wg_ref[...].astype(bf16)
        wub_ref[...] = wu_ref[...].astype(bf16)

    x = x_ref[...]
    hg = jnp.dot(x, wgb_ref[...], preferred_element_type=f32)
    hu = jnp.dot(x, wub_ref[...], preferred_element_type=f32)
    tm, tf = hg.shape
    cw = cw_ref[...]
    row = lax.broadcasted_iota(jnp.int32, (tm, tf), 0)
    r1 = pltpu.roll(hg, 1, axis=0)
    r2 = pltpu.roll(hg, 2, axis=0)

    def finish(t, hm1, hm2):
        p1 = jnp.where(t == 0, hm1, r1)
        p2 = jnp.where(t == 0, hm2, jnp.where(t == 1, hm1, r2))
        hc = cb_ref[...] + (cw[0:1, :] * p2 + cw[1:2, :] * p1 + cw[2:3, :] * hg)
        act = 0.5 * hc * (1.0 + lax.erf(hc * 0.7071067811865476))
        o_ref[...] = (act * hu).astype(o_ref.dtype)

    @pl.when(i < n_ptiles)
    def _():
        @pl.when(i % tiles_per_seq == 0)
        def _():
            carry_ref[...] = c0p_ref[0]
        prev = carry_ref[...]
        finish(row, prev[1:2, :], prev[0:1, :])
        carry_ref[...] = hg[tm - 2:tm, :]

    @pl.when(i >= n_ptiles)
    def _():
        c0 = c0s_ref[...]
        nsq = tm // ts
        hm1 = jnp.broadcast_to(c0[:, 1:2, :], (nsq, ts, tf)).reshape(tm, tf)
        hm2 = jnp.broadcast_to(c0[:, 0:1, :], (nsq, ts, tf)).reshape(tm, tf)
        finish(row % ts, hm1, hm2)


def _ffn_up_conv(xn, w_up, layer, c0p, c0s, cw, cb, mp, tp, ts):
    m, k = xn.shape
    f = w_up.shape[2] // 2
    ms = m - mp
    tm = next(t for t in (1024, 512, 256, 128, 64, 32, 16, 8) if tp % t == 0 and ms % t == 0 and t % ts == 0)
    tf = _pick(f, (256, 128))
    nf = f // tf
    n_ptiles = mp // tm
    tiles_per_seq = tp // tm
    nsq = tm // ts
    bp = c0p.shape[0]
    body = functools.partial(_upconv_body, n_ptiles=n_ptiles, tiles_per_seq=tiles_per_seq, ts=ts)
    return pl.pallas_call(
        body,
        grid=(nf, m // tm),
        in_specs=[pl.BlockSpec((tm, k), lambda j, i: (i, 0)),
                  pl.BlockSpec((None, k, tf), lambda j, i: (layer, 0, j)),
                  pl.BlockSpec((None, k, tf), lambda j, i: (layer, 0, nf + j)),
                  pl.BlockSpec((1, 2, tf), lambda j, i: (jnp.minimum(i // tiles_per_seq, bp - 1), 0, j)),
                  pl.BlockSpec((nsq, 2, tf), lambda j, i: (jnp.maximum(i - n_ptiles, 0), 0, j)),
                  pl.BlockSpec((3, tf), lambda j, i: (0, j)),
                  pl.BlockSpec((1, tf), lambda j, i: (0, j))],
        out_specs=pl.BlockSpec((tm, tf), lambda j, i: (i, j)),
        out_shape=jax.ShapeDtypeStruct((m, f), bf16),
        scratch_shapes=[pltpu.VMEM((k, tf), bf16), pltpu.VMEM((k, tf), bf16), pltpu.VMEM((2, tf), f32)],
        compiler_params=_params(("parallel", "arbitrary")),
        name="ffn_up_conv",
    )(xn, w_up, w_up, c0p, c0s, cw, cb.reshape(1, f))


def _first(accs, tiles, rows):
    return accs[0]


def _resid(accs, tiles, rows):
    return tiles[0] + accs[0]


def _bias(accs, tiles, rows):
    return accs[0] + rows[0]


def _ple(accs, tiles, rows):
    return tiles[0] + jax.nn.sigmoid(accs[0]) * accs[1]


def _tanh(accs, tiles, rows):
    return jnp.tanh(accs[0])


def _sigm(accs, tiles, rows):
    return jax.nn.sigmoid(accs[0])


def kernel(x_prompt, x_sample, p_prompt, p_sample, state_hgrn, state_rwkv, state_shift, state_ffn_conv, norm_mix, norm_ffn, norm_ple, norm_final, hg_w_in, hg_lb_logits, hg_gnorm, hg_w_o, rw_mu, rw_w_rkv, rw_w0, rw_w1, rw_w2, rw_a0, rw_a1, rw_a2, rw_g1, rw_g2, rw_k_k, rw_k_a, rw_r_k, rw_lnx_w, rw_lnx_b, rw_w_o, ffn_w_up, ffn_conv_w, ffn_conv_b, ffn_w_down, ple_w_proj, ple_w_gate):
    bp, tp, d = x_prompt.shape
    bs, ts, _ = x_sample.shape
    depth = norm_mix.shape[0]
    mp, ms = bp * tp, bs * ts
    groups = ((0, bp, tp), (mp, bs, ts))
    f_ff = ffn_conv_b.shape[1]
    w_down_bf = ffn_w_down.astype(bf16)

    h = jnp.concatenate([x_prompt.reshape(mp, d), x_sample.reshape(ms, d)], axis=0)
    p_all = jnp.concatenate([p_prompt.reshape(depth, mp, -1), p_sample.reshape(depth, ms, -1)], axis=1).astype(bf16)
    lb_all = jnp.cumsum(jax.nn.softmax(hg_lb_logits.astype(f32), axis=0), axis=0)

    hg_states = [[], []]
    rw_states = [[], []]
    sh_states = [[], []]
    cv_states = [[], []]
    for i in range(depth):
        j = i // 2
        if i % 2 == 0:
            xn = _rmsnorm(h, norm_mix[i], bf16)
            z = _mm([(xn, hg_w_in, j)], _first, f32, name="hgrn_in")
            mix_in = jnp.zeros((mp + ms, d), bf16)
            for gi, (off, nseq, t) in enumerate(groups):
                s0 = jnp.zeros((nseq, d // HG_HEAD, HG_HEAD, HG_HEAD), f32) if gi == 0 else state_hgrn[j].astype(f32)
                mix_in, s_new = _hgrn(z, lb_all[j], hg_gnorm[j], s0, off, nseq, t, mix_in)
                hg_states[gi].append(s_new)
            h = _mm([(mix_in, hg_w_o, j)], _resid, f32, tiles=(h,), name="hgrn_out")
        else:
            xr, xw, xk, xv, xa, xg = _norm_shift_mix(h, norm_mix[i], rw_mu[j], jnp.zeros((bp, d), f32),
                                                     state_shift[j].astype(f32), mp, tp, ts)
            h_last = jnp.concatenate([h[tp - 1:mp:tp], h[mp + ts - 1::ts]], axis=0)
            xn_last = _rmsnorm(h_last, norm_mix[i], f32)
            sh_states[0].append(xn_last[:bp])
            sh_states[1].append(xn_last[bp:])
            r = _mm([(xr, rw_w_rkv, j)], _first, f32, ncols=d, name="rwkv_r")
            k = _mm([(xk, rw_w_rkv, j)], _first, f32, ncols=d, col_off=d, name="rwkv_k")
            v = _mm([(xv, rw_w_rkv, j)], _first, f32, ncols=d, col_off=2 * d, name="rwkv_v")
            w_lin = _mm([(_mm([(xw, rw_w1, j)], _tanh, bf16, name="rwkv_w1"), rw_w2, j)],
                        _bias, f32, rows=(rw_w0[j],), name="rwkv_w2")
            a_lin = _mm([(_mm([(xa, rw_a1, j)], _first, bf16, name="rwkv_a1"), rw_a2, j)],
                        _bias, f32, rows=(rw_a0[j],), name="rwkv_a2")
            g = _mm([(_mm([(xg, rw_g1, j)], _sigm, bf16, name="rwkv_g1"), rw_g2, j)],
                    _first, f32, name="rwkv_g2")
            vecs = (rw_k_k[j], rw_k_a[j], rw_r_k[j].reshape(d), rw_lnx_w[j], rw_lnx_b[j])
            mix_in = jnp.zeros((mp + ms, d), bf16)
            for gi, (off, nseq, t) in enumerate(groups):
                s0 = jnp.zeros((nseq, d // RW_HEAD, RW_HEAD, RW_HEAD), f32) if gi == 0 else state_rwkv[j].astype(f32)
                mix_in, s_new = _rwkv((r, k, v, w_lin, a_lin, g), vecs, s0, off, nseq, t, mix_in)
                rw_states[gi].append(s_new)
            h = _mm([(mix_in, rw_w_o, j)], _resid, f32, tiles=(h,), name="rwkv_out")

        xn = _rmsnorm(h, norm_ffn[i], bf16)
        act = _ffn_up_conv(xn, ffn_w_up, i, jnp.zeros((bp, 2, f_ff), f32), state_ffn_conv[i].astype(f32),
                           ffn_conv_w[i], ffn_conv_b[i], mp, tp, ts)
        x_last = jnp.concatenate([xn[:mp].reshape(bp, tp, d)[:, tp - 2:].reshape(2 * bp, d),
                                  xn[mp:].reshape(bs, ts, d)[:, ts - 2:].reshape(2 * bs, d)], axis=0)
        c_new = _mm([(x_last, ffn_w_up, i)], _first, f32, ncols=f_ff, name="ffn_conv_state")
        cv_states[0].append(c_new[:2 * bp].reshape(bp, 2, f_ff))
        cv_states[1].append(c_new[2 * bp:].reshape(bs, 2, f_ff))
        h = _mm([(act, w_down_bf, i)], _resid, f32, tiles=(h,), tm=512, tn=256, stationary=False, name="ffn_down")

        xn = _rmsnorm(h, norm_ple[i], bf16)
        h = _mm([(xn, ple_w_gate, i), (p_all[i], ple_w_proj, i)], _ple, f32, tiles=(h,), name="ple")

    y_prompt = _rmsnorm(h, norm_final, f32, 0, mp).reshape(bp, tp, d)
    y_sample = _rmsnorm(h, norm_final, f32, mp, ms).reshape(bs, ts, d)
    st = lambda xs: jnp.stack(xs)
    return (y_prompt, y_sample,
            st(hg_states[0]), st(rw_states[0]), st(sh_states[0]), st(cv_states[0]),
            st(hg_states[1]), st(rw_states[1]), st(sh_states[1]), st(cv_states[1]))
```

```python
import functools

import numpy as np
import jax
import jax.numpy as jnp
from jax import lax
from jax.experimental import pallas as pl
from jax.experimental.pallas import tpu as pltpu

f32 = jnp.float32
bf16 = jnp.bfloat16

RMS_EPS = 1e-6
GN_EPS = 64e-5
HG_HEAD = 128
HG_CHUNK = 128
HG_HEADS_PER_STEP = 4
RW_HEAD = 64
RW_CHUNK = 64
RW_PAIRS_PER_STEP = 8
VMEM_LIMIT = 56 * 1024 * 1024


def _pick(n, prefs):
    for p in prefs:
        if n % p == 0:
            return p
    return n


def _params(sem):
    return pltpu.CompilerParams(dimension_semantics=sem, vmem_limit_bytes=VMEM_LIMIT)


def _nn(a, b):
    return jnp.dot(a.astype(bf16), b.astype(bf16), preferred_element_type=f32)


def _nt(a, b):
    return lax.dot_general(a.astype(bf16), b.astype(bf16), (((1,), (1,)), ((), ())), preferred_element_type=f32)


def _tn(a, b):
    return lax.dot_general(a.astype(bf16), b.astype(bf16), (((0,), (0,)), ((), ())), preferred_element_type=f32)


def _d(a, b):
    return jnp.dot(a, b, preferred_element_type=f32)


def _dnt(a, b):
    return lax.dot_general(a, b, (((1,), (1,)), ((), ())), preferred_element_type=f32)


def _dtn(a, b):
    return lax.dot_general(a, b, (((0,), (0,)), ((), ())), preferred_element_type=f32)


def _split2(x):
    hi = x.astype(bf16)
    lo = (x - hi.astype(f32)).astype(bf16)
    return jnp.concatenate([hi, lo], axis=1)


def _sum2(d2, w):
    return d2[:, 0:w] + d2[:, w:2 * w]


def _round_robin(gens):
    for _ in zip(*gens):
        pass


def _group_call(body, grid, in_specs, ins, out_specs, out_shapes, dst, scratch, sem, name):
    return pl.pallas_call(
        body,
        grid=grid, in_specs=list(in_specs) + [pl.BlockSpec(memory_space=pl.ANY)],
        out_specs=out_specs, out_shape=out_shapes,
        scratch_shapes=scratch, input_output_aliases={len(ins): 0},
        compiler_params=_params(sem), name=name,
    )(*ins, dst)


def _rms_body(x_ref, g_ref, o_ref):
    x = x_ref[...]
    ms = jnp.mean(x * x, axis=-1, keepdims=True)
    o_ref[...] = (x * lax.rsqrt(ms + RMS_EPS) * g_ref[...]).astype(o_ref.dtype)


def _rmsnorm(x, g, out_dtype, row_off=0, nrows=None):
    d = x.shape[1]
    m = x.shape[0] if nrows is None else nrows
    tm = _pick(m, (256, 128, 64, 32, 16, 8))
    off = row_off // tm
    return pl.pallas_call(
        _rms_body,
        grid=(m // tm,),
        in_specs=[pl.BlockSpec((tm, d), lambda i: (off + i, 0)), pl.BlockSpec((1, d), lambda i: (0, 0))],
        out_specs=pl.BlockSpec((tm, d), lambda i: (i, 0)),
        out_shape=jax.ShapeDtypeStruct((m, d), out_dtype),
        compiler_params=_params(("parallel",)),
        name="rmsnorm",
    )(x, g.reshape(1, d))


def _mm_body(*refs, nd, nt, nr, epi, stationary):
    dots = refs[: 2 * nd]
    tiles = refs[2 * nd: 2 * nd + nt]
    rows = refs[2 * nd + nt: 2 * nd + nt + nr]
    o_ref = refs[2 * nd + nt + nr]
    if stationary:
        wbs = refs[2 * nd + nt + nr + 1:]

        @pl.when(pl.program_id(1) == 0)
        def _():
            for q in range(nd):
                wbs[q][...] = dots[2 * q + 1][...].astype(bf16)
        ws = [wb[...] for wb in wbs]
    else:
        ws = [dots[2 * q + 1][...] for q in range(nd)]
    accs = [jnp.dot(dots[2 * q][...], ws[q], preferred_element_type=f32) for q in range(nd)]
    o_ref[...] = epi(accs, [t[...] for t in tiles], [r[...] for r in rows]).astype(o_ref.dtype)


def _mm(dots, epi, out_dtype, tiles=(), rows=(), tm=1024, tn=512, ncols=None, col_off=0, stationary=True, name="mm"):
    m = dots[0][0].shape[0]
    n = dots[0][1].shape[2] if ncols is None else ncols
    tm = _pick(m, (tm, 512, 256, 128, 64, 32, 16, 8))
    tn = _pick(n, (tn, 512, 256, 128))
    assert col_off % tn == 0
    joff = col_off // tn
    if stationary:
        grid = (n // tn, m // tm)
        ij = lambda a, b: (b, a)
    else:
        grid = (m // tm, n // tn)
        ij = lambda a, b: (a, b)
    ins, specs, scratch = [], [], []
    for x, w, layer in dots:
        k = x.shape[1]
        ins += [x, w]
        specs += [pl.BlockSpec((tm, k), lambda a, b: (ij(a, b)[0], 0)),
                  pl.BlockSpec((None, k, tn), lambda a, b, _l=layer: (_l, 0, joff + ij(a, b)[1]))]
        if stationary:
            scratch.append(pltpu.VMEM((k, tn), bf16))
    for t in tiles:
        ins.append(t)
        specs.append(pl.BlockSpec((tm, tn), lambda a, b: ij(a, b)))
    for r in rows:
        ins.append(r.reshape(1, n))
        specs.append(pl.BlockSpec((1, tn), lambda a, b: (0, ij(a, b)[1])))
    body = functools.partial(_mm_body, nd=len(dots), nt=len(tiles), nr=len(rows), epi=epi, stationary=stationary)
    return pl.pallas_call(
        body,
        grid=grid,
        in_specs=specs,
        out_specs=pl.BlockSpec((tm, tn), lambda a, b: ij(a, b)),
        out_shape=jax.ShapeDtypeStruct((m, n), out_dtype),
        scratch_shapes=scratch,
        compiler_params=_params(("parallel", "arbitrary")),
        name=name,
    )(*ins)


def _hgrn_consts(c):
    idx = np.arange(c)
    mats = [(idx[:, None] >= idx[None, :])]
    masks = [np.eye(c, dtype=bool)]
    b = 1
    while b < c:
        blk = idx // (2 * b)
        second = (idx % (2 * b)) >= b
        bnd = blk * 2 * b + b - 1
        lq = (idx[None, :] > bnd[:, None]) & (idx[None, :] <= idx[:, None]) & second[:, None]
        lk = (idx[None, :] > idx[:, None]) & (idx[None, :] <= bnd[:, None]) & (~second)[:, None]
        mats.append(lq | lk)
        masks.append((blk[:, None] == blk[None, :]) & second[:, None] & (~second)[None, :])
        b *= 2
    lmat = np.concatenate(mats, axis=0).astype(np.float32)
    return jnp.asarray(lmat, dtype=bf16), jnp.asarray(np.stack(masks).astype(np.float32)), len(mats) - 1


def _hgrn_body(q_ref, f_ref, i_ref, g_ref, lb_ref, gn_ref, s0_ref, l_ref, mask_ref, dst_ref, o_ref, so_ref, st_ref, *,
               c, nl, nchunks, nh):
    del dst_ref
    tb = pl.program_id(2)
    w = HG_HEAD

    @pl.when(tb == 0)
    def _():
        for hh in range(nh):
            st_ref[hh] = s0_ref[0, hh].T

    gn = gn_ref[...]

    def chunk(ci, carry):
        r0 = pl.multiple_of(ci * c, c)
        rows = pl.ds(r0, c)
        def head_steps(hh):
            cols = slice(hh * w, (hh + 1) * w)
            lb = lb_ref[:, cols]
            q = q_ref[rows, cols]
            fg = lb + (1.0 - lb) * jax.nn.sigmoid(f_ref[rows, cols])
            gl = jnp.log(fg)
            kk = 1.0 - fg
            qs = q * jax.nn.sigmoid(q)
            v = i_ref[rows, cols]
            d = _sum2(jnp.dot(l_ref[...], _split2(gl), preferred_element_type=f32), w)
            a = mask_ref[0] * _nt(qs, kk)
            yield
            gcum = d[0:c]
            gend = gcum[c - 1:c, :]
            st = st_ref[hh]
            oi = _nt(qs * jnp.exp(gcum), st)
            st_ref[hh] = st * jnp.exp(gend) + _tn(v, kk * jnp.exp(gend - gcum))
            for l in range(nl):
                e = jnp.exp(d[(1 + l) * c:(2 + l) * c])
                a = a + mask_ref[1 + l] * _nt(qs * e, kk * e)
            yield
            o = _nn(a, v) + oi
            yield
            ms = jnp.mean(o * o, axis=-1, keepdims=True)
            on = o * lax.rsqrt(ms + RMS_EPS) * gn
            o_ref[rows, cols] = (on * jax.nn.sigmoid(g_ref[rows, cols])).astype(o_ref.dtype)
            yield

        _round_robin([head_steps(hh) for hh in range(nh)])
        return carry

    lax.fori_loop(0, nchunks, chunk, 0)

    @pl.when(tb == pl.num_programs(2) - 1)
    def _():
        for hh in range(nh):
            so_ref[0, hh] = st_ref[hh].T


def _hgrn(z, lb, gnorm, s0, row_off, nseq, t, dst):
    m = z.shape[0]
    d = z.shape[1] // 4
    heads = d // HG_HEAD
    c = min(HG_CHUNK, t)
    nh = _pick(heads, (HG_HEADS_PER_STEP, 2, 1))
    tblk = _pick(t, (512, 256, 128, 64, 32))
    nblk = t // tblk
    off = row_off // tblk
    hgrp = heads // nh
    wblk = nh * HG_HEAD
    lmat, masks, nl = _hgrn_consts(c)

    def zspec(sec):
        return pl.BlockSpec((tblk, wblk), lambda s, h, b: (off + s * nblk + b, sec * hgrp + h))

    sspec = pl.BlockSpec((1, nh, HG_HEAD, HG_HEAD), lambda s, h, b: (s, h, 0, 0))
    body = functools.partial(_hgrn_body, c=c, nl=nl, nchunks=tblk // c, nh=nh)
    return _group_call(
        body, (nseq, hgrp, nblk),
        [zspec(0), zspec(1), zspec(2), zspec(3),
         pl.BlockSpec((1, wblk), lambda s, h, b: (0, h)),
         pl.BlockSpec((1, HG_HEAD), lambda s, h, b: (0, 0)),
         sspec,
         pl.BlockSpec(lmat.shape, lambda s, h, b: (0, 0)),
         pl.BlockSpec(masks.shape, lambda s, h, b: (0, 0, 0))],
        [z, z, z, z, lb.reshape(1, d), gnorm.reshape(1, HG_HEAD), s0, lmat, masks],
        [pl.BlockSpec((tblk, wblk), lambda s, h, b: (off + s * nblk + b, h)), sspec],
        [jax.ShapeDtypeStruct((m, d), bf16), jax.ShapeDtypeStruct((nseq, heads, HG_HEAD, HG_HEAD), f32)],
        dst, [pltpu.VMEM((nh, HG_HEAD, HG_HEAD), f32)], ("parallel", "parallel", "arbitrary"), "hgrn2_scan")


def _norm_shift_body(h_ref, halo_ref, g_ref, shp_ref, shs_ref, mu_ref, *o_refs, n_ptiles, tiles_per_seq, ts):
    i = pl.program_id(0)
    g = g_ref[...]

    def norm(v):
        return v * lax.rsqrt(jnp.mean(v * v, axis=-1, keepdims=True) + RMS_EPS) * g

    x = norm(h_ref[...])
    tm, d = x.shape
    row = lax.broadcasted_iota(jnp.int32, (tm, d), 0)
    rolled = pltpu.roll(x, 1, axis=0)

    def finish(xp):
        xx = xp - x
        for j, o_ref in enumerate(o_refs):
            o_ref[...] = (x + xx * mu_ref[j:j + 1, :]).astype(o_ref.dtype)

    @pl.when(i < n_ptiles)
    def _():
        first = jnp.where(i % tiles_per_seq == 0, shp_ref[0], norm(halo_ref[...])[7:8, :])
        finish(jnp.where(row == 0, first, rolled))

    @pl.when(i >= n_ptiles)
    def _():
        nsq = tm // ts
        first = jnp.broadcast_to(shs_ref[...], (nsq, ts, d)).reshape(tm, d)
        finish(jnp.where(row % ts == 0, first, rolled))


def _norm_shift_mix(h, g, mu, shift_p, shift_s, mp, tp, ts):
    m, d = h.shape
    ms = m - mp
    bp, bs = shift_p.shape[0], shift_s.shape[0]
    tm = next(t for t in (128, 64, 32, 16, 8) if tp % t == 0 and ms % t == 0 and t % ts == 0)
    n_ptiles = mp // tm
    tiles_per_seq = tp // tm
    nsq = tm // ts
    hb = tm // 8
    body = functools.partial(_norm_shift_body, n_ptiles=n_ptiles, tiles_per_seq=tiles_per_seq, ts=ts)
    return pl.pallas_call(
        body,
        grid=(m // tm,),
        in_specs=[pl.BlockSpec((tm, d), lambda i: (i, 0)),
                  pl.BlockSpec((8, d), lambda i: (jnp.maximum(i * hb - 1, 0), 0)),
                  pl.BlockSpec((1, d), lambda i: (0, 0)),
                  pl.BlockSpec((1, 1, d), lambda i: (jnp.minimum(i // tiles_per_seq, bp - 1), 0, 0)),
                  pl.BlockSpec((nsq, 1, d), lambda i: (jnp.maximum(i - n_ptiles, 0), 0, 0)),
                  pl.BlockSpec((6, d), lambda i: (0, 0))],
        out_specs=[pl.BlockSpec((tm, d), lambda i: (i, 0))] * 6,
        out_shape=[jax.ShapeDtypeStruct((m, d), bf16)] * 6,
        compiler_params=_params(("parallel",)),
        name="rwkv_norm_shift_mix",
    )(h, h, g.reshape(1, d), shift_p.reshape(bp, 1, d), shift_s.reshape(bs, 1, d), mu)


def _rwkv_body(r_ref, k_ref, v_ref, tw_ref, ta_ref, tg_ref, w2_ref, a2_ref, g2_ref, kk_ref, ka_ref, rk_ref, lnw_ref,
               lnb_ref, w0_ref, a0_ref, s0_ref, dst_ref, o_ref, so_ref, st_ref, w2b_ref, a2b_ref, g2b_ref, *,
               c, nchunks, npair):
    del dst_ref
    tb = pl.program_id(2)
    n = RW_HEAD
    w = 2 * n
    c2 = 2 * c
    i32 = jnp.int32
    w2b_ref[...] = w2_ref[...].astype(bf16)
    a2b_ref[...] = a2_ref[...].astype(bf16)
    g2b_ref[...] = g2_ref[...].astype(bf16)

    @pl.when(tb == 0)
    def _():
        st_ref[...] = jnp.zeros(st_ref.shape, f32)
        for p in range(npair):
            st_ref[p, 0:n, 0:n] = s0_ref[0, 2 * p]
            st_ref[p, n:w, n:w] = s0_ref[0, 2 * p + 1]

    ri = lax.broadcasted_iota(i32, (c2, c2), 0)
    cj = lax.broadcasted_iota(i32, (c2, c2), 1)
    same = (ri // c) == (cj // c)
    low_bd = (same & (ri >= cj)).astype(f32)
    slow_bd = (same & (ri > cj)).astype(f32)
    eye = (ri == cj).astype(f32)
    lvl = []
    b = 1
    while b < c:
        lvl.append(((ri // (2 * b) == cj // (2 * b)) & (ri % (2 * b) >= b) & (cj % (2 * b) < b)).astype(f32))
        b *= 2
    lvl_b = [m.astype(bf16) for m in lvl]
    lowc = (lax.broadcasted_iota(i32, (c, c), 0) >= lax.broadcasted_iota(i32, (c, c), 1)).astype(bf16)
    m0 = lax.broadcasted_iota(i32, (c, w), 1) < n
    hmask = (lax.broadcasted_iota(i32, (c2, w), 0) // c) == (lax.broadcasted_iota(i32, (c2, w), 1) // n)

    lane0 = m0.astype(bf16)
    lane1 = 1.0 - lane0

    def stack(x_b):
        return jnp.concatenate([x_b * lane0, x_b * lane1], axis=0)

    def halfsums(x):
        s0_ = jnp.sum(jnp.where(m0, x, 0.0), axis=-1, keepdims=True)
        s1_ = jnp.sum(jnp.where(m0, 0.0, x), axis=-1, keepdims=True)
        return s0_, s1_

    def chunk(ci, carry):
        r0 = pl.multiple_of(ci * c, c)
        rows = pl.ds(r0, c)
        tw = tw_ref[rows, :]
        ta = ta_ref[rows, :]
        tg = tg_ref[rows, :]

        def pair_steps(p):
            cols = slice(p * w, (p + 1) * w)
            r = r_ref[rows, cols]
            k = k_ref[rows, cols]
            v = v_ref[rows, cols]
            w_lin = _d(tw, w2b_ref[:, cols]) + w0_ref[:, cols]
            a_lin = _d(ta, a2b_ref[:, cols]) + a0_ref[:, cols]
            gate = _d(tg, g2b_ref[:, cols])
            yield
            lw = -jnp.exp(-jax.nn.softplus(-w_lin) - 0.5)
            a = jax.nn.sigmoid(a_lin)
            kkp = k * kk_ref[:, cols]
            k2 = k * (1.0 + (a - 1.0) * ka_ref[:, cols])
            cum = _sum2(jnp.dot(lowc, _split2(lw), preferred_element_type=f32), w)
            yield
            cend = cum[c - 1:c, :]
            e_incl = jnp.exp(cum)
            e_inv = jnp.exp(-cum)
            e_end = jnp.exp(cend - cum)
            gam = jnp.exp(cend)
            n0, n1 = halfsums(kkp * kkp)
            rn0 = 1.0 / jnp.maximum(jnp.sqrt(n0), 1e-12)
            rn1 = 1.0 / jnp.maximum(jnp.sqrt(n1), 1e-12)
            kkn = kkp * jnp.where(m0, rn0, rn1)
            ka_ = kkn * a
            kkt_b = stack((kkn * jnp.exp(cum - lw)).astype(bf16))
            bh_b = stack((ka_ * e_inv).astype(bf16))
            bb_b = stack((ka_ * e_end).astype(bf16))
            rt_b = stack((r * e_incl).astype(bf16))
            kh_b = stack((k2 * e_inv).astype(bf16))
            kb_b = stack((k2 * e_end).astype(bf16))
            v_b = stack(v.astype(bf16))
            gram = _dnt(jnp.concatenate([kkt_b, rt_b], axis=0), jnp.concatenate([kh_b, bh_b], axis=0))
            yield
            mb = slow_bd * gram[0:c2, c2:2 * c2]
            mk_b = (slow_bd * gram[0:c2, 0:c2]).astype(bf16)
            pk_b = (low_bd * gram[c2:2 * c2, 0:c2]).astype(bf16)
            pb_b = (low_bd * gram[c2:2 * c2, c2:2 * c2]).astype(bf16)
            mkv = _d(mk_b, v_b)
            pkv = _d(pk_b, v_b)
            vk = _dtn(v_b, kb_b)
            tm = eye - lvl[0] * mb
            mb_b = mb.astype(bf16)
            for l in range(1, len(lvl)):
                tm_b = tm.astype(bf16)
                tc = _d(tm_b, mb_b * lvl_b[l])
                yield
                tm = tm - _d(tc.astype(bf16), tm_b)
                yield
            wu = _d(tm.astype(bf16), jnp.concatenate([kkt_b, mkv.astype(bf16)], axis=1))
            yield
            s = st_ref[p]
            s_b = s.astype(bf16)
            u_b = (_dnt(wu[:, 0:w].astype(bf16), s_b) + wu[:, w:2 * w]).astype(bf16)
            yr = _dnt(rt_b, s_b) + pkv
            yield
            y_s = yr - _d(pb_b, u_b)
            st_ref[p] = s * gam + vk - _dtn(u_b, bb_b)
            yield
            mean = jnp.sum(y_s, axis=-1, keepdims=True) * (1.0 / n)
            yc = jnp.where(hmask, y_s - mean, 0.0)
            var = jnp.sum(yc * yc, axis=-1, keepdims=True) * (1.0 / n)
            yn_s = yc * lax.rsqrt(var + GN_EPS)
            yn = (yn_s[0:c] + yn_s[c:c2]) * lnw_ref[:, cols] + lnb_ref[:, cols]
            b0, b1 = halfsums(r * k2 * rk_ref[:, cols])
            bonus = jnp.where(m0, b0, b1) * v
            o_ref[rows, cols] = ((yn + bonus) * gate).astype(o_ref.dtype)
            yield

        _round_robin([pair_steps(p) for p in range(npair)])
        return carry

    lax.fori_loop(0, nchunks, chunk, 0)

    @pl.when(tb == pl.num_programs(2) - 1)
    def _():
        for p in range(npair):
            so_ref[0, 2 * p] = st_ref[p, 0:n, 0:n]
            so_ref[0, 2 * p + 1] = st_ref[p, n:w, n:w]


def _rwkv(acts, hidden, weights2, layer, vecs, s0, row_off, nseq, t, dst):
    m, d = acts[0].shape
    heads = d // RW_HEAD
    pairs = heads // 2
    npair = _pick(pairs, (RW_PAIRS_PER_STEP, 4, 2, 1))
    c = min(RW_CHUNK, t)
    tblk = _pick(t, (256, 128, 64, 32))
    nblk = t // tblk
    off = row_off // tblk
    wblk = npair * 2 * RW_HEAD
    aspec = pl.BlockSpec((tblk, wblk), lambda s, h, b: (off + s * nblk + b, h))
    hspecs = [pl.BlockSpec((tblk, x.shape[1]), lambda s, h, b: (off + s * nblk + b, 0)) for x in hidden]
    wspecs = [pl.BlockSpec((None, x.shape[1], wblk), lambda s, h, b: (layer, 0, h)) for x in weights2]
    vspec = pl.BlockSpec((1, wblk), lambda s, h, b: (0, h))
    sspec = pl.BlockSpec((1, 2 * npair, RW_HEAD, RW_HEAD), lambda s, h, b: (s, h, 0, 0))
    body = functools.partial(_rwkv_body, c=c, nchunks=tblk // c, npair=npair)
    return _group_call(
        body, (nseq, pairs // npair, nblk),
        [aspec] * 3 + hspecs + wspecs + [vspec] * 7 + [sspec],
        list(acts) + list(hidden) + list(weights2) + [x.reshape(1, d) for x in vecs] + [s0],
        [aspec, sspec],
        [jax.ShapeDtypeStruct((m, d), bf16), jax.ShapeDtypeStruct((nseq, heads, RW_HEAD, RW_HEAD), f32)],
        dst,
        [pltpu.VMEM((npair, 2 * RW_HEAD, 2 * RW_HEAD), f32)] + [pltpu.VMEM((x.shape[1], wblk), bf16) for x in weights2],
        ("parallel", "parallel", "arbitrary"), "rwkv7_scan")


def _upconv_body(x_ref, wg_ref, wu_ref, c0p_ref, c0s_ref, cw_ref, cb_ref, o_ref, cnp_ref, cns_ref,
                 wgb_ref, wub_ref, carry_ref, *, n_ptiles, tiles_per_seq, ts):
    i = pl.program_id(1)

    @pl.when(i == 0)
    def _():
        wgb_ref[...] = wg_ref[...].astype(bf16)
        wub_ref[...] = wu_ref[...].astype(bf16)

    x = x_ref[...]
    hg = jnp.dot(x, wgb_ref[...], preferred_element_type=f32)
    hu = jnp.dot(x, wub_ref[...], preferred_element_type=f32)
    tm, tf = hg.shape
    cw = cw_ref[...]
    row = lax.broadcasted_iota(jnp.int32, (tm, tf), 0)
    r1 = pltpu.roll(hg, 1, axis=0)
    r2 = pltpu.roll(hg, 2, axis=0)

    def finish(t, hm1, hm2):
        p1 = jnp.where(t == 0, hm1, r1)
        p2 = jnp.where(t == 0, hm2, jnp.where(t == 1, hm1, r2))
        hc = cb_ref[...] + (cw[0:1, :] * p2 + cw[1:2, :] * p1 + cw[2:3, :] * hg)
        act = 0.5 * hc * (1.0 + lax.erf(hc * 0.7071067811865476))
        o_ref[...] = (act * hu).astype(o_ref.dtype)

    @pl.when(i < n_ptiles)
    def _():
        @pl.when(i % tiles_per_seq == 0)
        def _():
            carry_ref[...] = c0p_ref[0]
        prev = carry_ref[...]
        finish(row, prev[1:2, :], prev[0:1, :])
        carry_ref[...] = hg[tm - 2:tm, :]

        @pl.when(i % tiles_per_seq == tiles_per_seq - 1)
        def _():
            cnp_ref[0] = hg[tm - 2:tm, :]

    @pl.when(i >= n_ptiles)
    def _():
        c0 = c0s_ref[...]
        nsq = tm // ts
        hm1 = jnp.broadcast_to(c0[:, 1:2, :], (nsq, ts, tf)).reshape(tm, tf)
        hm2 = jnp.broadcast_to(c0[:, 0:1, :], (nsq, ts, tf)).reshape(tm, tf)
        finish(row % ts, hm1, hm2)
        cns_ref[...] = hg.reshape(nsq, ts, tf)[:, ts - 2:ts, :]


def _ffn_up_conv(xn, w_up, layer, c0p, c0s, cw, cb, mp, tp, ts):
    m, k = xn.shape
    f = w_up.shape[2] // 2
    ms = m - mp
    tm = next(t for t in (1024, 512, 256, 128, 64, 32, 16, 8) if tp % t == 0 and ms % t == 0 and t % ts == 0)
    tf = _pick(f, (256, 128))
    nf = f // tf
    n_ptiles = mp // tm
    tiles_per_seq = tp // tm
    nsq = tm // ts
    bp = c0p.shape[0]
    body = functools.partial(_upconv_body, n_ptiles=n_ptiles, tiles_per_seq=tiles_per_seq, ts=ts)
    return pl.pallas_call(
        body,
        grid=(nf, m // tm),
        in_specs=[pl.BlockSpec((tm, k), lambda j, i: (i, 0)),
                  pl.BlockSpec((None, k, tf), lambda j, i: (layer, 0, j)),
                  pl.BlockSpec((None, k, tf), lambda j, i: (layer, 0, nf + j)),
                  pl.BlockSpec((1, 2, tf), lambda j, i: (jnp.minimum(i // tiles_per_seq, bp - 1), 0, j)),
                  pl.BlockSpec((nsq, 2, tf), lambda j, i: (jnp.maximum(i - n_ptiles, 0), 0, j)),
                  pl.BlockSpec((3, tf), lambda j, i: (0, j)),
                  pl.BlockSpec((1, tf), lambda j, i: (0, j))],
        out_specs=[pl.BlockSpec((tm, tf), lambda j, i: (i, j)),
                   pl.BlockSpec((1, 2, tf), lambda j, i: (jnp.minimum(i // tiles_per_seq, bp - 1), 0, j)),
                   pl.BlockSpec((nsq, 2, tf), lambda j, i: (jnp.maximum(i - n_ptiles, 0), 0, j))],
        out_shape=[jax.ShapeDtypeStruct((m, f), bf16),
                   jax.ShapeDtypeStruct((bp, 2, f), f32),
                   jax.ShapeDtypeStruct((c0s.shape[0], 2, f), f32)],
        scratch_shapes=[pltpu.VMEM((k, tf), bf16), pltpu.VMEM((k, tf), bf16), pltpu.VMEM((2, tf), f32)],
        compiler_params=_params(("parallel", "arbitrary")),
        name="ffn_up_conv",
    )(xn, w_up, w_up, c0p, c0s, cw, cb.reshape(1, f))


def _first(accs, tiles, rows):
    return accs[0]


def _resid(accs, tiles, rows):
    return tiles[0] + accs[0]


def _ple(accs, tiles, rows):
    return tiles[0] + jax.nn.sigmoid(accs[0]) * accs[1]


def _tanh(accs, tiles, rows):
    return jnp.tanh(accs[0])


def _sigm(accs, tiles, rows):
    return jax.nn.sigmoid(accs[0])


def kernel(x_prompt, x_sample, p_prompt, p_sample, state_hgrn, state_rwkv, state_shift, state_ffn_conv, norm_mix, norm_ffn, norm_ple, norm_final, hg_w_in, hg_lb_logits, hg_gnorm, hg_w_o, rw_mu, rw_w_rkv, rw_w0, rw_w1, rw_w2, rw_a0, rw_a1, rw_a2, rw_g1, rw_g2, rw_k_k, rw_k_a, rw_r_k, rw_lnx_w, rw_lnx_b, rw_w_o, ffn_w_up, ffn_conv_w, ffn_conv_b, ffn_w_down, ple_w_proj, ple_w_gate):
    bp, tp, d = x_prompt.shape
    bs, ts, _ = x_sample.shape
    depth = norm_mix.shape[0]
    mp, ms = bp * tp, bs * ts
    groups = ((0, bp, tp), (mp, bs, ts))
    f_ff = ffn_conv_b.shape[1]
    w_down_bf = ffn_w_down.astype(bf16)

    h = jnp.concatenate([x_prompt.reshape(mp, d), x_sample.reshape(ms, d)], axis=0)
    p_all = jnp.concatenate([p_prompt.reshape(depth, mp, -1), p_sample.reshape(depth, ms, -1)], axis=1).astype(bf16)
    lb_all = jnp.cumsum(jax.nn.softmax(hg_lb_logits.astype(f32), axis=0), axis=0)

    hg_states = [[], []]
    rw_states = [[], []]
    sh_states = [[], []]
    cv_states = [[], []]
    for i in range(depth):
        j = i // 2
        if i % 2 == 0:
            xn = _rmsnorm(h, norm_mix[i], bf16)
            z = _mm([(xn, hg_w_in, j)], _first, f32, name="hgrn_in")
            mix_in = jnp.zeros((mp + ms, d), bf16)
            for gi, (off, nseq, t) in enumerate(groups):
                s0 = jnp.zeros((nseq, d // HG_HEAD, HG_HEAD, HG_HEAD), f32) if gi == 0 else state_hgrn[j].astype(f32)
                mix_in, s_new = _hgrn(z, lb_all[j], hg_gnorm[j], s0, off, nseq, t, mix_in)
                hg_states[gi].append(s_new)
            h = _mm([(mix_in, hg_w_o, j)], _resid, f32, tiles=(h,), name="hgrn_out")
        else:
            xr, xw, xk, xv, xa, xg = _norm_shift_mix(h, norm_mix[i], rw_mu[j], jnp.zeros((bp, d), f32),
                                                     state_shift[j].astype(f32), mp, tp, ts)
            h_last = jnp.concatenate([h[tp - 1:mp:tp], h[mp + ts - 1::ts]], axis=0)
            xn_last = _rmsnorm(h_last, norm_mix[i], f32)
            sh_states[0].append(xn_last[:bp])
            sh_states[1].append(xn_last[bp:])
            r = _mm([(xr, rw_w_rkv, j)], _first, f32, ncols=d, name="rwkv_r")
            k = _mm([(xk, rw_w_rkv, j)], _first, f32, ncols=d, col_off=d, name="rwkv_k")
            v = _mm([(xv, rw_w_rkv, j)], _first, f32, ncols=d, col_off=2 * d, name="rwkv_v")
            hidden = (_mm([(xw, rw_w1, j)], _tanh, bf16, name="rwkv_w1"),
                      _mm([(xa, rw_a1, j)], _first, bf16, name="rwkv_a1"),
                      _mm([(xg, rw_g1, j)], _sigm, bf16, name="rwkv_g1"))
            vecs = (rw_k_k[j], rw_k_a[j], rw_r_k[j].reshape(d), rw_lnx_w[j], rw_lnx_b[j], rw_w0[j], rw_a0[j])
            mix_in = jnp.zeros((mp + ms, d), bf16)
            for gi, (off, nseq, t) in enumerate(groups):
                s0 = jnp.zeros((nseq, d // RW_HEAD, RW_HEAD, RW_HEAD), f32) if gi == 0 else state_rwkv[j].astype(f32)
                mix_in, s_new = _rwkv((r, k, v), hidden, (rw_w2, rw_a2, rw_g2), j, vecs, s0, off, nseq, t, mix_in)
                rw_states[gi].append(s_new)
            h = _mm([(mix_in, rw_w_o, j)], _resid, f32, tiles=(h,), name="rwkv_out")

        xn = _rmsnorm(h, norm_ffn[i], bf16)
        act, cn_p, cn_s = _ffn_up_conv(xn, ffn_w_up, i, jnp.zeros((bp, 2, f_ff), f32),
                                       state_ffn_conv[i].astype(f32), ffn_conv_w[i], ffn_conv_b[i], mp, tp, ts)
        cv_states[0].append(cn_p)
        cv_states[1].append(cn_s)
        h = _mm([(act, w_down_bf, i)], _resid, f32, tiles=(h,), tm=512, tn=256, stationary=False, name="ffn_down")

        xn = _rmsnorm(h, norm_ple[i], bf16)
        h = _mm([(xn, ple_w_gate, i), (p_all[i], ple_w_proj, i)], _ple, f32, tiles=(h,), name="ple")

    y_prompt = _rmsnorm(h, norm_final, f32, 0, mp).reshape(bp, tp, d)
    y_sample = _rmsnorm(h, norm_final, f32, mp, ms).reshape(bs, ts, d)
    st = lambda xs: jnp.stack(xs)
    return (y_prompt, y_sample,
            st(hg_states[0]), st(rw_states[0]), st(sh_states[0]), st(cv_states[0]),
            st(hg_states[1]), st(rw_states[1]), st(sh_states[1]), st(cv_states[1]))
```

```python
import functools

import numpy as np
import jax
import jax.numpy as jnp
from jax import lax
from jax.experimental import pallas as pl
from jax.experimental.pallas import tpu as pltpu

f32 = jnp.float32
bf16 = jnp.bfloat16

RMS_EPS = 1e-6
GN_EPS = 64e-5
HG_HEAD = 128
HG_CHUNK = 128
HG_HEADS_PER_STEP = 4
RW_HEAD = 64
RW_CHUNK = 64
RW_PAIRS_PER_STEP = 8
VMEM_LIMIT = 56 * 1024 * 1024


def _pick(n, prefs):
    for p in prefs:
        if n % p == 0:
            return p
    return n


def _params(sem):
    return pltpu.CompilerParams(dimension_semantics=sem, vmem_limit_bytes=VMEM_LIMIT)


def _nn(a, b):
    return jnp.dot(a.astype(bf16), b.astype(bf16), preferred_element_type=f32)


def _nt(a, b):
    return lax.dot_general(a.astype(bf16), b.astype(bf16), (((1,), (1,)), ((), ())), preferred_element_type=f32)


def _tn(a, b):
    return lax.dot_general(a.astype(bf16), b.astype(bf16), (((0,), (0,)), ((), ())), preferred_element_type=f32)


def _d(a, b):
    return jnp.dot(a, b, preferred_element_type=f32)


def _dnt(a, b):
    return lax.dot_general(a, b, (((1,), (1,)), ((), ())), preferred_element_type=f32)


def _dtn(a, b):
    return lax.dot_general(a, b, (((0,), (0,)), ((), ())), preferred_element_type=f32)


def _split2(x):
    hi = x.astype(bf16)
    lo = (x - hi.astype(f32)).astype(bf16)
    return jnp.concatenate([hi, lo], axis=1)


def _sum2(d2, w):
    return d2[:, 0:w] + d2[:, w:2 * w]


def _round_robin(gens):
    for _ in zip(*gens):
        pass


def _group_call(body, grid, in_specs, ins, out_specs, out_shapes, dst, scratch, sem, name):
    return pl.pallas_call(
        body,
        grid=grid, in_specs=list(in_specs) + [pl.BlockSpec(memory_space=pl.ANY)],
        out_specs=out_specs, out_shape=out_shapes,
        scratch_shapes=scratch, input_output_aliases={len(ins): 0},
        compiler_params=_params(sem), name=name,
    )(*ins, dst)


def _rms_body(x_ref, g_ref, o_ref):
    x = x_ref[...]
    ms = jnp.mean(x * x, axis=-1, keepdims=True)
    o_ref[...] = (x * lax.rsqrt(ms + RMS_EPS) * g_ref[...]).astype(o_ref.dtype)


def _rmsnorm(x, g, out_dtype, row_off=0, nrows=None):
    d = x.shape[1]
    m = x.shape[0] if nrows is None else nrows
    tm = _pick(m, (256, 128, 64, 32, 16, 8))
    off = row_off // tm
    return pl.pallas_call(
        _rms_body,
        grid=(m // tm,),
        in_specs=[pl.BlockSpec((tm, d), lambda i: (off + i, 0)), pl.BlockSpec((1, d), lambda i: (0, 0))],
        out_specs=pl.BlockSpec((tm, d), lambda i: (i, 0)),
        out_shape=jax.ShapeDtypeStruct((m, d), out_dtype),
        compiler_params=_params(("parallel",)),
        name="rmsnorm",
    )(x, g.reshape(1, d))


def _mm_body(*refs, nd, nt, nr, epi, stationary):
    dots = refs[: 2 * nd]
    tiles = refs[2 * nd: 2 * nd + nt]
    rows = refs[2 * nd + nt: 2 * nd + nt + nr]
    o_ref = refs[2 * nd + nt + nr]
    if stationary:
        wbs = refs[2 * nd + nt + nr + 1:]

        @pl.when(pl.program_id(1) == 0)
        def _():
            for q in range(nd):
                wbs[q][...] = dots[2 * q + 1][...].astype(bf16)
        ws = [wb[...] for wb in wbs]
    else:
        ws = [dots[2 * q + 1][...] for q in range(nd)]
    accs = [jnp.dot(dots[2 * q][...], ws[q], preferred_element_type=f32) for q in range(nd)]
    o_ref[...] = epi(accs, [t[...] for t in tiles], [r[...] for r in rows]).astype(o_ref.dtype)


def _mm(dots, epi, out_dtype, tiles=(), rows=(), tm=1024, tn=512, ncols=None, col_off=0, stationary=True, name="mm"):
    m = dots[0][0].shape[0]
    n = dots[0][1].shape[2] if ncols is None else ncols
    tm = _pick(m, (tm, 512, 256, 128, 64, 32, 16, 8))
    tn = _pick(n, (tn, 512, 256, 128))
    assert col_off % tn == 0
    joff = col_off // tn
    if stationary:
        grid = (n // tn, m // tm)
        ij = lambda a, b: (b, a)
    else:
        grid = (m // tm, n // tn)
        ij = lambda a, b: (a, b)
    ins, specs, scratch = [], [], []
    for x, w, layer in dots:
        k = x.shape[1]
        ins += [x, w]
        specs += [pl.BlockSpec((tm, k), lambda a, b: (ij(a, b)[0], 0)),
                  pl.BlockSpec((None, k, tn), lambda a, b, _l=layer: (_l, 0, joff + ij(a, b)[1]))]
        if stationary:
            scratch.append(pltpu.VMEM((k, tn), bf16))
    for t in tiles:
        ins.append(t)
        specs.append(pl.BlockSpec((tm, tn), lambda a, b: ij(a, b)))
    for r in rows:
        ins.append(r.reshape(1, n))
        specs.append(pl.BlockSpec((1, tn), lambda a, b: (0, ij(a, b)[1])))
    body = functools.partial(_mm_body, nd=len(dots), nt=len(tiles), nr=len(rows), epi=epi, stationary=stationary)
    return pl.pallas_call(
        body,
        grid=grid,
        in_specs=specs,
        out_specs=pl.BlockSpec((tm, tn), lambda a, b: ij(a, b)),
        out_shape=jax.ShapeDtypeStruct((m, n), out_dtype),
        scratch_shapes=scratch,
        compiler_params=_params(("parallel", "arbitrary")),
        name=name,
    )(*ins)


def _hgrn_consts(c):
    idx = np.arange(c)
    mats = [(idx[:, None] >= idx[None, :])]
    masks = [np.eye(c, dtype=bool)]
    b = 1
    while b < c:
        blk = idx // (2 * b)
        second = (idx % (2 * b)) >= b
        bnd = blk * 2 * b + b - 1
        lq = (idx[None, :] > bnd[:, None]) & (idx[None, :] <= idx[:, None]) & second[:, None]
        lk = (idx[None, :] > idx[:, None]) & (idx[None, :] <= bnd[:, None]) & (~second)[:, None]
        mats.append(lq | lk)
        masks.append((blk[:, None] == blk[None, :]) & second[:, None] & (~second)[None, :])
        b *= 2
    lmat = np.concatenate(mats, axis=0).astype(np.float32)
    return jnp.asarray(lmat, dtype=bf16), jnp.asarray(np.stack(masks).astype(np.float32)), len(mats) - 1


def _hgrn_body(q_ref, f_ref, i_ref, g_ref, lb_ref, gn_ref, s0_ref, l_ref, mask_ref, dst_ref, o_ref, so_ref, st_ref, *,
               c, nl, nchunks, nh):
    del dst_ref
    tb = pl.program_id(2)
    w = HG_HEAD

    @pl.when(tb == 0)
    def _():
        for hh in range(nh):
            st_ref[hh] = s0_ref[0, hh].T

    gn = gn_ref[...]

    def chunk(ci, carry):
        r0 = pl.multiple_of(ci * c, c)
        rows = pl.ds(r0, c)
        def head_steps(hh):
            cols = slice(hh * w, (hh + 1) * w)
            lb = lb_ref[:, cols]
            q = q_ref[rows, cols]
            fg = lb + (1.0 - lb) * jax.nn.sigmoid(f_ref[rows, cols])
            gl = jnp.log(fg)
            kk = 1.0 - fg
            qs = q * jax.nn.sigmoid(q)
            v = i_ref[rows, cols]
            d = _sum2(jnp.dot(l_ref[...], _split2(gl), preferred_element_type=f32), w)
            a = mask_ref[0] * _nt(qs, kk)
            yield
            gcum = d[0:c]
            gend = gcum[c - 1:c, :]
            st = st_ref[hh]
            oi = _nt(qs * jnp.exp(gcum), st)
            st_ref[hh] = st * jnp.exp(gend) + _tn(v, kk * jnp.exp(gend - gcum))
            for l in range(nl):
                e = jnp.exp(d[(1 + l) * c:(2 + l) * c])
                a = a + mask_ref[1 + l] * _nt(qs * e, kk * e)
            yield
            o = _nn(a, v) + oi
            yield
            ms = jnp.mean(o * o, axis=-1, keepdims=True)
            on = o * lax.rsqrt(ms + RMS_EPS) * gn
            o_ref[rows, cols] = (on * jax.nn.sigmoid(g_ref[rows, cols])).astype(o_ref.dtype)
            yield

        _round_robin([head_steps(hh) for hh in range(nh)])
        return carry

    lax.fori_loop(0, nchunks, chunk, 0)

    @pl.when(tb == pl.num_programs(2) - 1)
    def _():
        for hh in range(nh):
            so_ref[0, hh] = st_ref[hh].T


def _hgrn(z, lb, gnorm, s0, row_off, nseq, t, dst):
    m = z.shape[0]
    d = z.shape[1] // 4
    heads = d // HG_HEAD
    c = min(HG_CHUNK, t)
    nh = _pick(heads, (HG_HEADS_PER_STEP, 2, 1))
    tblk = _pick(t, (512, 256, 128, 64, 32))
    nblk = t // tblk
    off = row_off // tblk
    hgrp = heads // nh
    wblk = nh * HG_HEAD
    lmat, masks, nl = _hgrn_consts(c)

    def zspec(sec):
        return pl.BlockSpec((tblk, wblk), lambda s, h, b: (off + s * nblk + b, sec * hgrp + h))

    sspec = pl.BlockSpec((1, nh, HG_HEAD, HG_HEAD), lambda s, h, b: (s, h, 0, 0))
    body = functools.partial(_hgrn_body, c=c, nl=nl, nchunks=tblk // c, nh=nh)
    return _group_call(
        body, (nseq, hgrp, nblk),
        [zspec(0), zspec(1), zspec(2), zspec(3),
         pl.BlockSpec((1, wblk), lambda s, h, b: (0, h)),
         pl.BlockSpec((1, HG_HEAD), lambda s, h, b: (0, 0)),
         sspec,
         pl.BlockSpec(lmat.shape, lambda s, h, b: (0, 0)),
         pl.BlockSpec(masks.shape, lambda s, h, b: (0, 0, 0))],
        [z, z, z, z, lb.reshape(1, d), gnorm.reshape(1, HG_HEAD), s0, lmat, masks],
        [pl.BlockSpec((tblk, wblk), lambda s, h, b: (off + s * nblk + b, h)), sspec],
        [jax.ShapeDtypeStruct((m, d), bf16), jax.ShapeDtypeStruct((nseq, heads, HG_HEAD, HG_HEAD), f32)],
        dst, [pltpu.VMEM((nh, HG_HEAD, HG_HEAD), f32)], ("parallel", "parallel", "arbitrary"), "hgrn2_scan")


def _norm_shift_body(h_ref, halo_ref, g_ref, shp_ref, shs_ref, mu_ref, w1_ref, a1_ref, g1_ref,
                     or_ref, ok_ref, ov_ref, tw_ref, ta_ref, tg_ref, w1b_ref, a1b_ref, g1b_ref, *,
                     n_ptiles, tiles_per_seq, ts):
    i = pl.program_id(0)
    g = g_ref[...]

    @pl.when(i == 0)
    def _():
        w1b_ref[...] = w1_ref[...].astype(bf16)
        a1b_ref[...] = a1_ref[...].astype(bf16)
        g1b_ref[...] = g1_ref[...].astype(bf16)

    def norm(v):
        return v * lax.rsqrt(jnp.mean(v * v, axis=-1, keepdims=True) + RMS_EPS) * g

    x = norm(h_ref[...])
    tm, d = x.shape
    row = lax.broadcasted_iota(jnp.int32, (tm, d), 0)
    rolled = pltpu.roll(x, 1, axis=0)

    def finish(xp):
        xx = xp - x
        mix = lambda j: (x + xx * mu_ref[j:j + 1, :]).astype(bf16)
        or_ref[...] = mix(0)
        ok_ref[...] = mix(2)
        ov_ref[...] = mix(3)
        tw_ref[...] = jnp.tanh(_d(mix(1), w1b_ref[...])).astype(bf16)
        ta_ref[...] = _d(mix(4), a1b_ref[...]).astype(bf16)
        tg_ref[...] = jax.nn.sigmoid(_d(mix(5), g1b_ref[...])).astype(bf16)

    @pl.when(i < n_ptiles)
    def _():
        first = jnp.where(i % tiles_per_seq == 0, shp_ref[0], norm(halo_ref[...])[7:8, :])
        finish(jnp.where(row == 0, first, rolled))

    @pl.when(i >= n_ptiles)
    def _():
        nsq = tm // ts
        first = jnp.broadcast_to(shs_ref[...], (nsq, ts, d)).reshape(tm, d)
        finish(jnp.where(row % ts == 0, first, rolled))


def _norm_shift_mix(h, g, mu, shift_p, shift_s, low1, layer, mp, tp, ts):
    m, d = h.shape
    ranks = [x.shape[2] for x in low1]
    ms = m - mp
    bp, bs = shift_p.shape[0], shift_s.shape[0]
    tm = next(t for t in (128, 64, 32, 16, 8) if tp % t == 0 and ms % t == 0 and t % ts == 0)
    n_ptiles = mp // tm
    tiles_per_seq = tp // tm
    nsq = tm // ts
    hb = tm // 8
    body = functools.partial(_norm_shift_body, n_ptiles=n_ptiles, tiles_per_seq=tiles_per_seq, ts=ts)
    return pl.pallas_call(
        body,
        grid=(m // tm,),
        in_specs=[pl.BlockSpec((tm, d), lambda i: (i, 0)),
                  pl.BlockSpec((8, d), lambda i: (jnp.maximum(i * hb - 1, 0), 0)),
                  pl.BlockSpec((1, d), lambda i: (0, 0)),
                  pl.BlockSpec((1, 1, d), lambda i: (jnp.minimum(i // tiles_per_seq, bp - 1), 0, 0)),
                  pl.BlockSpec((nsq, 1, d), lambda i: (jnp.maximum(i - n_ptiles, 0), 0, 0)),
                  pl.BlockSpec((6, d), lambda i: (0, 0))]
                 + [pl.BlockSpec((None, d, r), lambda i: (layer, 0, 0)) for r in ranks],
        out_specs=[pl.BlockSpec((tm, d), lambda i: (i, 0))] * 3 + [pl.BlockSpec((tm, r), lambda i: (i, 0)) for r in ranks],
        out_shape=[jax.ShapeDtypeStruct((m, d), bf16)] * 3 + [jax.ShapeDtypeStruct((m, r), bf16) for r in ranks],
        scratch_shapes=[pltpu.VMEM((d, r), bf16) for r in ranks],
        compiler_params=_params(("arbitrary",)),
        name="rwkv_norm_shift_mix",
    )(h, h, g.reshape(1, d), shift_p.reshape(bp, 1, d), shift_s.reshape(bs, 1, d), mu, *low1)


def _rwkv_body(r_ref, k_ref, v_ref, tw_ref, ta_ref, tg_ref, w2_ref, a2_ref, g2_ref, kk_ref, ka_ref, rk_ref, lnw_ref,
               lnb_ref, w0_ref, a0_ref, s0_ref, dst_ref, o_ref, so_ref, st_ref, w2b_ref, a2b_ref, g2b_ref, *,
               c, nchunks, npair):
    del dst_ref
    tb = pl.program_id(2)
    n = RW_HEAD
    w = 2 * n
    c2 = 2 * c
    i32 = jnp.int32
    w2b_ref[...] = w2_ref[...].astype(bf16)
    a2b_ref[...] = a2_ref[...].astype(bf16)
    g2b_ref[...] = g2_ref[...].astype(bf16)

    @pl.when(tb == 0)
    def _():
        st_ref[...] = jnp.zeros(st_ref.shape, f32)
        for p in range(npair):
            st_ref[p, 0:n, 0:n] = s0_ref[0, 2 * p]
            st_ref[p, n:w, n:w] = s0_ref[0, 2 * p + 1]

    ri = lax.broadcasted_iota(i32, (c2, c2), 0)
    cj = lax.broadcasted_iota(i32, (c2, c2), 1)
    same = (ri // c) == (cj // c)
    low_bd = (same & (ri >= cj)).astype(f32)
    slow_bd = (same & (ri > cj)).astype(f32)
    eye = (ri == cj).astype(f32)
    lvl = []
    b = 1
    while b < c:
        lvl.append(((ri // (2 * b) == cj // (2 * b)) & (ri % (2 * b) >= b) & (cj % (2 * b) < b)).astype(f32))
        b *= 2
    lvl_b = [m.astype(bf16) for m in lvl]
    lowc = (lax.broadcasted_iota(i32, (c, c), 0) >= lax.broadcasted_iota(i32, (c, c), 1)).astype(bf16)
    m0 = lax.broadcasted_iota(i32, (c, w), 1) < n
    hmask = (lax.broadcasted_iota(i32, (c2, w), 0) // c) == (lax.broadcasted_iota(i32, (c2, w), 1) // n)

    lane0 = m0.astype(bf16)
    lane1 = 1.0 - lane0

    def stack(x_b):
        return jnp.concatenate([x_b * lane0, x_b * lane1], axis=0)

    def halfsums(x):
        s0_ = jnp.sum(jnp.where(m0, x, 0.0), axis=-1, keepdims=True)
        s1_ = jnp.sum(jnp.where(m0, 0.0, x), axis=-1, keepdims=True)
        return s0_, s1_

    def chunk(ci, carry):
        r0 = pl.multiple_of(ci * c, c)
        rows = pl.ds(r0, c)
        tw = tw_ref[rows, :]
        ta = ta_ref[rows, :]
        tg = tg_ref[rows, :]

        def pair_steps(p):
            cols = slice(p * w, (p + 1) * w)
            r = r_ref[rows, cols]
            k = k_ref[rows, cols]
            v = v_ref[rows, cols]
            w_lin = _d(tw, w2b_ref[:, cols]) + w0_ref[:, cols]
            a_lin = _d(ta, a2b_ref[:, cols]) + a0_ref[:, cols]
            gate = _d(tg, g2b_ref[:, cols])
            yield
            lw = -jnp.exp(-jax.nn.softplus(-w_lin) - 0.5)
            a = jax.nn.sigmoid(a_lin)
            kkp = k * kk_ref[:, cols]
            k2 = k * (1.0 + (a - 1.0) * ka_ref[:, cols])
            cum = _sum2(jnp.dot(lowc, _split2(lw), preferred_element_type=f32), w)
            yield
            cend = cum[c - 1:c, :]
            e_incl = jnp.exp(cum)
            e_inv = jnp.exp(-cum)
            e_end = jnp.exp(cend - cum)
            gam = jnp.exp(cend)
            n0, n1 = halfsums(kkp * kkp)
            rn0 = 1.0 / jnp.maximum(jnp.sqrt(n0), 1e-12)
            rn1 = 1.0 / jnp.maximum(jnp.sqrt(n1), 1e-12)
            kkn = kkp * jnp.where(m0, rn0, rn1)
            ka_ = kkn * a
            kkt_b = stack((kkn * jnp.exp(cum - lw)).astype(bf16))
            bh_b = stack((ka_ * e_inv).astype(bf16))
            bb_b = stack((ka_ * e_end).astype(bf16))
            rt_b = stack((r * e_incl).astype(bf16))
            kh_b = stack((k2 * e_inv).astype(bf16))
            kb_b = stack((k2 * e_end).astype(bf16))
            v_b = stack(v.astype(bf16))
            gram = _dnt(jnp.concatenate([kkt_b, rt_b], axis=0), jnp.concatenate([kh_b, bh_b], axis=0))
            yield
            mb = slow_bd * gram[0:c2, c2:2 * c2]
            mk_b = (slow_bd * gram[0:c2, 0:c2]).astype(bf16)
            pk_b = (low_bd * gram[c2:2 * c2, 0:c2]).astype(bf16)
            pb_b = (low_bd * gram[c2:2 * c2, c2:2 * c2]).astype(bf16)
            mkv = _d(mk_b, v_b)
            pkv = _d(pk_b, v_b)
            vk = _dtn(v_b, kb_b)
            tm = eye - lvl[0] * mb
            mb_b = mb.astype(bf16)
            for l in range(1, len(lvl)):
                tm_b = tm.astype(bf16)
                tc = _d(tm_b, mb_b * lvl_b[l])
                yield
                tm = tm - _d(tc.astype(bf16), tm_b)
                yield
            wu = _d(tm.astype(bf16), jnp.concatenate([kkt_b, mkv.astype(bf16)], axis=1))
            yield
            s = st_ref[p]
            s_b = s.astype(bf16)
            u_b = (_dnt(wu[:, 0:w].astype(bf16), s_b) + wu[:, w:2 * w]).astype(bf16)
            yr = _dnt(rt_b, s_b) + pkv
            yield
            y_s = yr - _d(pb_b, u_b)
            st_ref[p] = s * gam + vk - _dtn(u_b, bb_b)
            yield
            mean = jnp.sum(y_s, axis=-1, keepdims=True) * (1.0 / n)
            yc = jnp.where(hmask, y_s - mean, 0.0)
            var = jnp.sum(yc * yc, axis=-1, keepdims=True) * (1.0 / n)
            yn_s = yc * lax.rsqrt(var + GN_EPS)
            yn = (yn_s[0:c] + yn_s[c:c2]) * lnw_ref[:, cols] + lnb_ref[:, cols]
            b0, b1 = halfsums(r * k2 * rk_ref[:, cols])
            bonus = jnp.where(m0, b0, b1) * v
            o_ref[rows, cols] = ((yn + bonus) * gate).astype(o_ref.dtype)
            yield

        _round_robin([pair_steps(p) for p in range(npair)])
        return carry

    lax.fori_loop(0, nchunks, chunk, 0)

    @pl.when(tb == pl.num_programs(2) - 1)
    def _():
        for p in range(npair):
            so_ref[0, 2 * p] = st_ref[p, 0:n, 0:n]
            so_ref[0, 2 * p + 1] = st_ref[p, n:w, n:w]


def _rwkv(acts, hidden, weights2, layer, vecs, s0, row_off, nseq, t, dst):
    m, d = acts[0].shape
    heads = d // RW_HEAD
    pairs = heads // 2
    npair = _pick(pairs, (RW_PAIRS_PER_STEP, 4, 2, 1))
    c = min(RW_CHUNK, t)
    tblk = _pick(t, (256, 128, 64, 32))
    nblk = t // tblk
    off = row_off // tblk
    wblk = npair * 2 * RW_HEAD
    aspec = pl.BlockSpec((tblk, wblk), lambda s, h, b: (off + s * nblk + b, h))
    hspecs = [pl.BlockSpec((tblk, x.shape[1]), lambda s, h, b: (off + s * nblk + b, 0)) for x in hidden]
    wspecs = [pl.BlockSpec((None, x.shape[1], wblk), lambda s, h, b: (layer, 0, h)) for x in weights2]
    vspec = pl.BlockSpec((1, wblk), lambda s, h, b: (0, h))
    sspec = pl.BlockSpec((1, 2 * npair, RW_HEAD, RW_HEAD), lambda s, h, b: (s, h, 0, 0))
    body = functools.partial(_rwkv_body, c=c, nchunks=tblk // c, npair=npair)
    return _group_call(
        body, (nseq, pairs // npair, nblk),
        [aspec] * 3 + hspecs + wspecs + [vspec] * 7 + [sspec],
        list(acts) + list(hidden) + list(weights2) + [x.reshape(1, d) for x in vecs] + [s0],
        [aspec, sspec],
        [jax.ShapeDtypeStruct((m, d), bf16), jax.ShapeDtypeStruct((nseq, heads, RW_HEAD, RW_HEAD), f32)],
        dst,
        [pltpu.VMEM((npair, 2 * RW_HEAD, 2 * RW_HEAD), f32)] + [pltpu.VMEM((x.shape[1], wblk), bf16) for x in weights2],
        ("parallel", "parallel", "arbitrary"), "rwkv7_scan")


def _upconv_body(x_ref, wg_ref, wu_ref, c0p_ref, c0s_ref, cw_ref, cb_ref, o_ref, cnp_ref, cns_ref,
                 wb_ref, carry_ref, *, n_ptiles, tiles_per_seq, ts):
    i = pl.program_id(1)
    tf = o_ref.shape[1]

    @pl.when(i == 0)
    def _():
        wb_ref[:, 0:tf] = wg_ref[...].astype(bf16)
        wb_ref[:, tf:2 * tf] = wu_ref[...].astype(bf16)

    both = jnp.dot(x_ref[...], wb_ref[...], preferred_element_type=f32)
    hg = both[:, 0:tf]
    hu = both[:, tf:2 * tf]
    tm = hg.shape[0]
    cw = cw_ref[...]
    row = lax.broadcasted_iota(jnp.int32, (tm, tf), 0)
    r1 = pltpu.roll(hg, 1, axis=0)
    r2 = pltpu.roll(hg, 2, axis=0)

    def gated(p2, p1, g0, u0):
        hc = cb_ref[...] + (cw[0:1, :] * p2 + cw[1:2, :] * p1 + cw[2:3, :] * g0)
        act = 0.5 * hc * (1.0 + lax.erf(hc * 0.7071067811865476))
        return (act * u0).astype(o_ref.dtype)

    def finish(t, hm1, hm2):
        p1 = jnp.where(t == 0, hm1, r1)
        p2 = jnp.where(t == 0, hm2, jnp.where(t == 1, hm1, r2))
        o_ref[...] = gated(p2, p1, hg, hu)

    @pl.when(i < n_ptiles)
    def _():
        @pl.when(i % tiles_per_seq == 0)
        def _():
            carry_ref[...] = c0p_ref[0]
        prev = carry_ref[...]
        finish(row, prev[1:2, :], prev[0:1, :])
        carry_ref[...] = hg[tm - 2:tm, :]

        @pl.when(i % tiles_per_seq == tiles_per_seq - 1)
        def _():
            cnp_ref[0] = hg[tm - 2:tm, :]

    @pl.when(i >= n_ptiles)
    def _():
        c0 = c0s_ref[...]
        nsq = tm // ts
        hm1 = jnp.broadcast_to(c0[:, 1:2, :], (nsq, ts, tf)).reshape(tm, tf)
        hm2 = jnp.broadcast_to(c0[:, 0:1, :], (nsq, ts, tf)).reshape(tm, tf)
        finish(row % ts, hm1, hm2)
        cns_ref[...] = hg.reshape(nsq, ts, tf)[:, ts - 2:ts, :]


def _ffn_up_conv(xn, w_up, layer, c0p, c0s, cw, cb, mp, tp, ts):
    m, k = xn.shape
    f = w_up.shape[2] // 2
    ms = m - mp
    tm = next(t for t in (1024, 512, 256, 128, 64, 32, 16, 8) if tp % t == 0 and ms % t == 0 and t % ts == 0)
    tf = _pick(f, (256, 128))
    nf = f // tf
    n_ptiles = mp // tm
    tiles_per_seq = tp // tm
    nsq = tm // ts
    bp = c0p.shape[0]
    body = functools.partial(_upconv_body, n_ptiles=n_ptiles, tiles_per_seq=tiles_per_seq, ts=ts)
    return pl.pallas_call(
        body,
        grid=(nf, m // tm),
        in_specs=[pl.BlockSpec((tm, k), lambda j, i: (i, 0)),
                  pl.BlockSpec((None, k, tf), lambda j, i: (layer, 0, j)),
                  pl.BlockSpec((None, k, tf), lambda j, i: (layer, 0, nf + j)),
                  pl.BlockSpec((1, 2, tf), lambda j, i: (jnp.minimum(i // tiles_per_seq, bp - 1), 0, j)),
                  pl.BlockSpec((nsq, 2, tf), lambda j, i: (jnp.maximum(i - n_ptiles, 0), 0, j)),
                  pl.BlockSpec((3, tf), lambda j, i: (0, j)),
                  pl.BlockSpec((1, tf), lambda j, i: (0, j))],
        out_specs=[pl.BlockSpec((tm, tf), lambda j, i: (i, j)),
                   pl.BlockSpec((1, 2, tf), lambda j, i: (jnp.minimum(i // tiles_per_seq, bp - 1), 0, j)),
                   pl.BlockSpec((nsq, 2, tf), lambda j, i: (jnp.maximum(i - n_ptiles, 0), 0, j))],
        out_shape=[jax.ShapeDtypeStruct((m, f), bf16),
                   jax.ShapeDtypeStruct((bp, 2, f), f32),
                   jax.ShapeDtypeStruct((c0s.shape[0], 2, f), f32)],
        scratch_shapes=[pltpu.VMEM((k, 2 * tf), bf16), pltpu.VMEM((2, tf), f32)],
        compiler_params=_params(("parallel", "arbitrary")),
        name="ffn_up_conv",
    )(xn, w_up, w_up, c0p, c0s, cw, cb.reshape(1, f))


def _first(accs, tiles, rows):
    return accs[0]


def _resid(accs, tiles, rows):
    return tiles[0] + accs[0]


def _ple(accs, tiles, rows):
    return tiles[0] + jax.nn.sigmoid(accs[0]) * accs[1]


def kernel(x_prompt, x_sample, p_prompt, p_sample, state_hgrn, state_rwkv, state_shift, state_ffn_conv, norm_mix, norm_ffn, norm_ple, norm_final, hg_w_in, hg_lb_logits, hg_gnorm, hg_w_o, rw_mu, rw_w_rkv, rw_w0, rw_w1, rw_w2, rw_a0, rw_a1, rw_a2, rw_g1, rw_g2, rw_k_k, rw_k_a, rw_r_k, rw_lnx_w, rw_lnx_b, rw_w_o, ffn_w_up, ffn_conv_w, ffn_conv_b, ffn_w_down, ple_w_proj, ple_w_gate):
    bp, tp, d = x_prompt.shape
    bs, ts, _ = x_sample.shape
    depth = norm_mix.shape[0]
    mp, ms = bp * tp, bs * ts
    groups = ((0, bp, tp), (mp, bs, ts))
    f_ff = ffn_conv_b.shape[1]
    w_down_bf = ffn_w_down.astype(bf16)

    h = jnp.concatenate([x_prompt.reshape(mp, d), x_sample.reshape(ms, d)], axis=0)
    p_all = jnp.concatenate([p_prompt.reshape(depth, mp, -1), p_sample.reshape(depth, ms, -1)], axis=1).astype(bf16)
    lb_all = jnp.cumsum(jax.nn.softmax(hg_lb_logits.astype(f32), axis=0), axis=0)

    hg_states = [[], []]
    rw_states = [[], []]
    sh_states = [[], []]
    cv_states = [[], []]
    for i in range(depth):
        j = i // 2
        if i % 2 == 0:
            xn = _rmsnorm(h, norm_mix[i], bf16)
            z = _mm([(xn, hg_w_in, j)], _first, f32, name="hgrn_in")
            mix_in = jnp.zeros((mp + ms, d), bf16)
            for gi, (off, nseq, t) in enumerate(groups):
                s0 = jnp.zeros((nseq, d // HG_HEAD, HG_HEAD, HG_HEAD), f32) if gi == 0 else state_hgrn[j].astype(f32)
                mix_in, s_new = _hgrn(z, lb_all[j], hg_gnorm[j], s0, off, nseq, t, mix_in)
                hg_states[gi].append(s_new)
            h = _mm([(mix_in, hg_w_o, j)], _resid, f32, tiles=(h,), name="hgrn_out")
        else:
            xr, xk, xv, *hidden = _norm_shift_mix(h, norm_mix[i], rw_mu[j], jnp.zeros((bp, d), f32),
                                                  state_shift[j].astype(f32), (rw_w1, rw_a1, rw_g1), j, mp, tp, ts)
            h_last = jnp.concatenate([h[tp - 1:mp:tp], h[mp + ts - 1::ts]], axis=0)
            xn_last = _rmsnorm(h_last, norm_mix[i], f32)
            sh_states[0].append(xn_last[:bp])
            sh_states[1].append(xn_last[bp:])
            r = _mm([(xr, rw_w_rkv, j)], _first, f32, ncols=d, name="rwkv_r")
            k = _mm([(xk, rw_w_rkv, j)], _first, f32, ncols=d, col_off=d, name="rwkv_k")
            v = _mm([(xv, rw_w_rkv, j)], _first, f32, ncols=d, col_off=2 * d, name="rwkv_v")
            vecs = (rw_k_k[j], rw_k_a[j], rw_r_k[j].reshape(d), rw_lnx_w[j], rw_lnx_b[j], rw_w0[j], rw_a0[j])
            mix_in = jnp.zeros((mp + ms, d), bf16)
            for gi, (off, nseq, t) in enumerate(groups):
                s0 = jnp.zeros((nseq, d // RW_HEAD, RW_HEAD, RW_HEAD), f32) if gi == 0 else state_rwkv[j].astype(f32)
                mix_in, s_new = _rwkv((r, k, v), hidden, (rw_w2, rw_a2, rw_g2), j, vecs, s0, off, nseq, t, mix_in)
                rw_states[gi].append(s_new)
            h = _mm([(mix_in, rw_w_o, j)], _resid, f32, tiles=(h,), name="rwkv_out")

        xn = _rmsnorm(h, norm_ffn[i], bf16)
        act, cn_p, cn_s = _ffn_up_conv(xn, ffn_w_up, i, jnp.zeros((bp, 2, f_ff), f32),
                                       state_ffn_conv[i].astype(f32), ffn_conv_w[i], ffn_conv_b[i], mp, tp, ts)
        cv_states[0].append(cn_p)
        cv_states[1].append(cn_s)
        h = _mm([(act, w_down_bf, i)], _resid, f32, tiles=(h,), tm=512, tn=512, stationary=False, name="ffn_down")

        xn = _rmsnorm(h, norm_ple[i], bf16)
        h = _mm([(xn, ple_w_gate, i), (p_all[i], ple_w_proj, i)], _ple, f32, tiles=(h,), name="ple")

    y_prompt = _rmsnorm(h, norm_final, f32, 0, mp).reshape(bp, tp, d)
    y_sample = _rmsnorm(h, norm_final, f32, mp, ms).reshape(bs, ts, d)
    st = lambda xs: jnp.stack(xs)
    return (y_prompt, y_sample,
            st(hg_states[0]), st(rw_states[0]), st(sh_states[0]), st(cv_states[0]),
            st(hg_states[1]), st(rw_states[1]), st(sh_states[1]), st(cv_states[1]))
```

```python
import functools

import numpy as np
import jax
import jax.numpy as jnp
from jax import lax
from jax.experimental import pallas as pl
from jax.experimental.pallas import tpu as pltpu

f32 = jnp.float32
bf16 = jnp.bfloat16

RMS_EPS = 1e-6
GN_EPS = 64e-5
HG_HEAD = 128
HG_CHUNK = 128
HG_HEADS_PER_STEP = 8
RW_HEAD = 64
RW_CHUNK = 64
RW_PAIRS_PER_STEP = 16
VMEM_LIMIT = 56 * 1024 * 1024


def _pick(n, prefs):
    for p in prefs:
        if n % p == 0:
            return p
    return n


def _params(sem):
    return pltpu.CompilerParams(dimension_semantics=sem, vmem_limit_bytes=VMEM_LIMIT)


def _nn(a, b):
    return jnp.dot(a.astype(bf16), b.astype(bf16), preferred_element_type=f32)


def _nt(a, b):
    return lax.dot_general(a.astype(bf16), b.astype(bf16), (((1,), (1,)), ((), ())), preferred_element_type=f32)


def _tn(a, b):
    return lax.dot_general(a.astype(bf16), b.astype(bf16), (((0,), (0,)), ((), ())), preferred_element_type=f32)


def _d(a, b):
    return jnp.dot(a, b, preferred_element_type=f32)


def _dnt(a, b):
    return lax.dot_general(a, b, (((1,), (1,)), ((), ())), preferred_element_type=f32)


def _dtn(a, b):
    return lax.dot_general(a, b, (((0,), (0,)), ((), ())), preferred_element_type=f32)


def _split2(x):
    hi = x.astype(bf16)
    lo = (x - hi.astype(f32)).astype(bf16)
    return jnp.concatenate([hi, lo], axis=1)


def _sum2(d2, w):
    return d2[:, 0:w] + d2[:, w:2 * w]


def _round_robin(gens):
    for _ in zip(*gens):
        pass


def _group_call(body, grid, in_specs, ins, out_specs, out_shapes, dst, scratch, sem, name):
    return pl.pallas_call(
        body,
        grid=grid, in_specs=list(in_specs) + [pl.BlockSpec(memory_space=pl.ANY)],
        out_specs=out_specs, out_shape=out_shapes,
        scratch_shapes=scratch, input_output_aliases={len(ins): 0},
        compiler_params=_params(sem), name=name,
    )(*ins, dst)


def _rms_body(x_ref, g_ref, o_ref):
    x = x_ref[...]
    ms = jnp.mean(x * x, axis=-1, keepdims=True)
    o_ref[...] = (x * lax.rsqrt(ms + RMS_EPS) * g_ref[...]).astype(o_ref.dtype)


def _rmsnorm(x, g, out_dtype, row_off=0, nrows=None):
    d = x.shape[1]
    m = x.shape[0] if nrows is None else nrows
    tm = _pick(m, (256, 128, 64, 32, 16, 8))
    off = row_off // tm
    return pl.pallas_call(
        _rms_body,
        grid=(m // tm,),
        in_specs=[pl.BlockSpec((tm, d), lambda i: (off + i, 0)), pl.BlockSpec((1, d), lambda i: (0, 0))],
        out_specs=pl.BlockSpec((tm, d), lambda i: (i, 0)),
        out_shape=jax.ShapeDtypeStruct((m, d), out_dtype),
        compiler_params=_params(("parallel",)),
        name="rmsnorm",
    )(x, g.reshape(1, d))


def _mm_body(*refs, nd, nt, nr, epi, stationary):
    dots = refs[: 2 * nd]
    tiles = refs[2 * nd: 2 * nd + nt]
    rows = refs[2 * nd + nt: 2 * nd + nt + nr]
    o_ref = refs[2 * nd + nt + nr]
    if stationary:
        wbs = refs[2 * nd + nt + nr + 1:]

        @pl.when(pl.program_id(1) == 0)
        def _():
            for q in range(nd):
                wbs[q][...] = dots[2 * q + 1][...].astype(bf16)
        ws = [wb[...] for wb in wbs]
    else:
        ws = [dots[2 * q + 1][...] for q in range(nd)]
    accs = [jnp.dot(dots[2 * q][...], ws[q], preferred_element_type=f32) for q in range(nd)]
    o_ref[...] = epi(accs, [t[...] for t in tiles], [r[...] for r in rows]).astype(o_ref.dtype)


def _mm(dots, epi, out_dtype, tiles=(), rows=(), tm=1024, tn=512, ncols=None, col_off=0, stationary=True, name="mm"):
    m = dots[0][0].shape[0]
    n = dots[0][1].shape[2] if ncols is None else ncols
    tm = _pick(m, (tm, 512, 256, 128, 64, 32, 16, 8))
    tn = _pick(n, (tn, 512, 256, 128))
    assert col_off % tn == 0
    joff = col_off // tn
    if stationary:
        grid = (n // tn, m // tm)
        ij = lambda a, b: (b, a)
    else:
        grid = (m // tm, n // tn)
        ij = lambda a, b: (a, b)
    ins, specs, scratch = [], [], []
    for x, w, layer in dots:
        k = x.shape[1]
        ins += [x, w]
        specs += [pl.BlockSpec((tm, k), lambda a, b: (ij(a, b)[0], 0)),
                  pl.BlockSpec((None, k, tn), lambda a, b, _l=layer: (_l, 0, joff + ij(a, b)[1]))]
        if stationary:
            scratch.append(pltpu.VMEM((k, tn), bf16))
    for t in tiles:
        ins.append(t)
        specs.append(pl.BlockSpec((tm, tn), lambda a, b: ij(a, b)))
    for r in rows:
        ins.append(r.reshape(1, n))
        specs.append(pl.BlockSpec((1, tn), lambda a, b: (0, ij(a, b)[1])))
    body = functools.partial(_mm_body, nd=len(dots), nt=len(tiles), nr=len(rows), epi=epi, stationary=stationary)
    return pl.pallas_call(
        body,
        grid=grid,
        in_specs=specs,
        out_specs=pl.BlockSpec((tm, tn), lambda a, b: ij(a, b)),
        out_shape=jax.ShapeDtypeStruct((m, n), out_dtype),
        scratch_shapes=scratch,
        compiler_params=_params(("parallel", "arbitrary")),
        name=name,
    )(*ins)


def _hgrn_consts(c):
    idx = np.arange(c)
    mats = [(idx[:, None] >= idx[None, :])]
    masks = [np.eye(c, dtype=bool)]
    b = 1
    while b < c:
        blk = idx // (2 * b)
        second = (idx % (2 * b)) >= b
        bnd = blk * 2 * b + b - 1
        lq = (idx[None, :] > bnd[:, None]) & (idx[None, :] <= idx[:, None]) & second[:, None]
        lk = (idx[None, :] > idx[:, None]) & (idx[None, :] <= bnd[:, None]) & (~second)[:, None]
        mats.append(lq | lk)
        masks.append((blk[:, None] == blk[None, :]) & second[:, None] & (~second)[None, :])
        b *= 2
    lmat = np.concatenate(mats, axis=0).astype(np.float32)
    return jnp.asarray(lmat, dtype=bf16), jnp.asarray(np.stack(masks).astype(np.float32)), len(mats) - 1


def _hgrn_body(q_ref, f_ref, i_ref, g_ref, lb_ref, gn_ref, s0_ref, l_ref, mask_ref, dst_ref, o_ref, so_ref, st_ref, *,
               c, nl, nchunks, nh):
    del dst_ref
    tb = pl.program_id(2)
    w = HG_HEAD

    @pl.when(tb == 0)
    def _():
        for hh in range(nh):
            st_ref[hh] = s0_ref[0, hh].T

    gn = gn_ref[...]

    def chunk(ci, carry):
        r0 = pl.multiple_of(ci * c, c)
        rows = pl.ds(r0, c)
        def head_steps(hh):
            cols = slice(hh * w, (hh + 1) * w)
            lb = lb_ref[:, cols]
            q = q_ref[rows, cols]
            fg = lb + (1.0 - lb) * jax.nn.sigmoid(f_ref[rows, cols])
            gl = jnp.log(fg)
            kk = 1.0 - fg
            qs = q * jax.nn.sigmoid(q)
            v = i_ref[rows, cols]
            d = _sum2(jnp.dot(l_ref[...], _split2(gl), preferred_element_type=f32), w)
            a = mask_ref[0] * _nt(qs, kk)
            yield
            gcum = d[0:c]
            gend = gcum[c - 1:c, :]
            st = st_ref[hh]
            oi = _nt(qs * jnp.exp(gcum), st)
            st_ref[hh] = st * jnp.exp(gend) + _tn(v, kk * jnp.exp(gend - gcum))
            for l in range(nl):
                e = jnp.exp(d[(1 + l) * c:(2 + l) * c])
                a = a + mask_ref[1 + l] * _nt(qs * e, kk * e)
            yield
            o = _nn(a, v) + oi
            yield
            ms = jnp.mean(o * o, axis=-1, keepdims=True)
            on = o * lax.rsqrt(ms + RMS_EPS) * gn
            o_ref[rows, cols] = (on * jax.nn.sigmoid(g_ref[rows, cols])).astype(o_ref.dtype)
            yield

        _round_robin([head_steps(hh) for hh in range(nh)])
        return carry

    lax.fori_loop(0, nchunks, chunk, 0)

    @pl.when(tb == pl.num_programs(2) - 1)
    def _():
        for hh in range(nh):
            so_ref[0, hh] = st_ref[hh].T


def _hgrn(z, lb, gnorm, s0, row_off, nseq, t, dst):
    m = z.shape[0]
    d = z.shape[1] // 4
    heads = d // HG_HEAD
    c = min(HG_CHUNK, t)
    nh = _pick(heads, (HG_HEADS_PER_STEP, 2, 1))
    tblk = _pick(t, (512, 256, 128, 64, 32))
    nblk = t // tblk
    off = row_off // tblk
    hgrp = heads // nh
    wblk = nh * HG_HEAD
    lmat, masks, nl = _hgrn_consts(c)

    def zspec(sec):
        return pl.BlockSpec((tblk, wblk), lambda s, h, b: (off + s * nblk + b, sec * hgrp + h))

    sspec = pl.BlockSpec((1, nh, HG_HEAD, HG_HEAD), lambda s, h, b: (s, h, 0, 0))
    body = functools.partial(_hgrn_body, c=c, nl=nl, nchunks=tblk // c, nh=nh)
    return _group_call(
        body, (nseq, hgrp, nblk),
        [zspec(0), zspec(1), zspec(2), zspec(3),
         pl.BlockSpec((1, wblk), lambda s, h, b: (0, h)),
         pl.BlockSpec((1, HG_HEAD), lambda s, h, b: (0, 0)),
         sspec,
         pl.BlockSpec(lmat.shape, lambda s, h, b: (0, 0)),
         pl.BlockSpec(masks.shape, lambda s, h, b: (0, 0, 0))],
        [z, z, z, z, lb.reshape(1, d), gnorm.reshape(1, HG_HEAD), s0, lmat, masks],
        [pl.BlockSpec((tblk, wblk), lambda s, h, b: (off + s * nblk + b, h)), sspec],
        [jax.ShapeDtypeStruct((m, d), bf16), jax.ShapeDtypeStruct((nseq, heads, HG_HEAD, HG_HEAD), f32)],
        dst, [pltpu.VMEM((nh, HG_HEAD, HG_HEAD), f32)], ("parallel", "parallel", "arbitrary"), "hgrn2_scan")


def _norm_shift_body(h_ref, halo_ref, g_ref, shp_ref, shs_ref, mu_ref, w1_ref, a1_ref, g1_ref,
                     or_ref, ok_ref, ov_ref, tw_ref, ta_ref, tg_ref, w1b_ref, a1b_ref, g1b_ref, *,
                     n_ptiles, tiles_per_seq, ts):
    i = pl.program_id(0)
    g = g_ref[...]

    @pl.when(i == 0)
    def _():
        w1b_ref[...] = w1_ref[...].astype(bf16)
        a1b_ref[...] = a1_ref[...].astype(bf16)
        g1b_ref[...] = g1_ref[...].astype(bf16)

    def norm(v):
        return v * lax.rsqrt(jnp.mean(v * v, axis=-1, keepdims=True) + RMS_EPS) * g

    x = norm(h_ref[...])
    tm, d = x.shape
    row = lax.broadcasted_iota(jnp.int32, (tm, d), 0)
    rolled = pltpu.roll(x, 1, axis=0)

    def finish(xp):
        xx = xp - x
        mix = lambda j: (x + xx * mu_ref[j:j + 1, :]).astype(bf16)
        or_ref[...] = mix(0)
        ok_ref[...] = mix(2)
        ov_ref[...] = mix(3)
        tw_ref[...] = jnp.tanh(_d(mix(1), w1b_ref[...])).astype(bf16)
        ta_ref[...] = _d(mix(4), a1b_ref[...]).astype(bf16)
        tg_ref[...] = jax.nn.sigmoid(_d(mix(5), g1b_ref[...])).astype(bf16)

    @pl.when(i < n_ptiles)
    def _():
        first = jnp.where(i % tiles_per_seq == 0, shp_ref[0], norm(halo_ref[...])[7:8, :])
        finish(jnp.where(row == 0, first, rolled))

    @pl.when(i >= n_ptiles)
    def _():
        nsq = tm // ts
        first = jnp.broadcast_to(shs_ref[...], (nsq, ts, d)).reshape(tm, d)
        finish(jnp.where(row % ts == 0, first, rolled))


def _norm_shift_mix(h, g, mu, shift_p, shift_s, low1, layer, mp, tp, ts):
    m, d = h.shape
    ranks = [x.shape[2] for x in low1]
    ms = m - mp
    bp, bs = shift_p.shape[0], shift_s.shape[0]
    tm = next(t for t in (128, 64, 32, 16, 8) if tp % t == 0 and ms % t == 0 and t % ts == 0)
    n_ptiles = mp // tm
    tiles_per_seq = tp // tm
    nsq = tm // ts
    hb = tm // 8
    body = functools.partial(_norm_shift_body, n_ptiles=n_ptiles, tiles_per_seq=tiles_per_seq, ts=ts)
    return pl.pallas_call(
        body,
        grid=(m // tm,),
        in_specs=[pl.BlockSpec((tm, d), lambda i: (i, 0)),
                  pl.BlockSpec((8, d), lambda i: (jnp.maximum(i * hb - 1, 0), 0)),
                  pl.BlockSpec((1, d), lambda i: (0, 0)),
                  pl.BlockSpec((1, 1, d), lambda i: (jnp.minimum(i // tiles_per_seq, bp - 1), 0, 0)),
                  pl.BlockSpec((nsq, 1, d), lambda i: (jnp.maximum(i - n_ptiles, 0), 0, 0)),
                  pl.BlockSpec((6, d), lambda i: (0, 0))]
                 + [pl.BlockSpec((None, d, r), lambda i: (layer, 0, 0)) for r in ranks],
        out_specs=[pl.BlockSpec((tm, d), lambda i: (i, 0))] * 3 + [pl.BlockSpec((tm, r), lambda i: (i, 0)) for r in ranks],
        out_shape=[jax.ShapeDtypeStruct((m, d), bf16)] * 3 + [jax.ShapeDtypeStruct((m, r), bf16) for r in ranks],
        scratch_shapes=[pltpu.VMEM((d, r), bf16) for r in ranks],
        compiler_params=_params(("arbitrary",)),
        name="rwkv_norm_shift_mix",
    )(h, h, g.reshape(1, d), shift_p.reshape(bp, 1, d), shift_s.reshape(bs, 1, d), mu, *low1)


def _rwkv_body(r_ref, k_ref, v_ref, tw_ref, ta_ref, tg_ref, w2_ref, a2_ref, g2_ref, kk_ref, ka_ref, rk_ref, lnw_ref,
               lnb_ref, w0_ref, a0_ref, s0_ref, dst_ref, o_ref, so_ref, st_ref, w2b_ref, a2b_ref, g2b_ref, *,
               c, nchunks, npair):
    del dst_ref
    tb = pl.program_id(2)
    n = RW_HEAD
    w = 2 * n
    c2 = 2 * c
    i32 = jnp.int32
    w2b_ref[...] = w2_ref[...].astype(bf16)
    a2b_ref[...] = a2_ref[...].astype(bf16)
    g2b_ref[...] = g2_ref[...].astype(bf16)

    @pl.when(tb == 0)
    def _():
        st_ref[...] = jnp.zeros(st_ref.shape, f32)
        for p in range(npair):
            st_ref[p, 0:n, 0:n] = s0_ref[0, 2 * p]
            st_ref[p, n:w, n:w] = s0_ref[0, 2 * p + 1]

    ri = lax.broadcasted_iota(i32, (c2, c2), 0)
    cj = lax.broadcasted_iota(i32, (c2, c2), 1)
    same = (ri // c) == (cj // c)
    low_bd = (same & (ri >= cj)).astype(f32)
    slow_bd = (same & (ri > cj)).astype(f32)
    eye = (ri == cj).astype(f32)
    lvl = []
    b = 1
    while b < c:
        lvl.append(((ri // (2 * b) == cj // (2 * b)) & (ri % (2 * b) >= b) & (cj % (2 * b) < b)).astype(f32))
        b *= 2
    lvl_b = [m.astype(bf16) for m in lvl]
    lowc = (lax.broadcasted_iota(i32, (c, c), 0) >= lax.broadcasted_iota(i32, (c, c), 1)).astype(bf16)
    m0 = lax.broadcasted_iota(i32, (c, w), 1) < n
    hmask = (lax.broadcasted_iota(i32, (c2, w), 0) // c) == (lax.broadcasted_iota(i32, (c2, w), 1) // n)

    lane0 = m0.astype(bf16)
    lane1 = 1.0 - lane0

    def stack(x_b):
        return jnp.concatenate([x_b * lane0, x_b * lane1], axis=0)

    def halfsums(x):
        s0_ = jnp.sum(jnp.where(m0, x, 0.0), axis=-1, keepdims=True)
        s1_ = jnp.sum(jnp.where(m0, 0.0, x), axis=-1, keepdims=True)
        return s0_, s1_

    def chunk(ci, carry):
        r0 = pl.multiple_of(ci * c, c)
        rows = pl.ds(r0, c)
        tw = tw_ref[rows, :]
        ta = ta_ref[rows, :]
        tg = tg_ref[rows, :]

        def pair_steps(p):
            cols = slice(p * w, (p + 1) * w)
            r = r_ref[rows, cols]
            k = k_ref[rows, cols]
            v = v_ref[rows, cols]
            w_lin = _d(tw, w2b_ref[:, cols]) + w0_ref[:, cols]
            a_lin = _d(ta, a2b_ref[:, cols]) + a0_ref[:, cols]
            gate = _d(tg, g2b_ref[:, cols])
            yield
            lw = -jnp.exp(-jax.nn.softplus(-w_lin) - 0.5)
            a = jax.nn.sigmoid(a_lin)
            kkp = k * kk_ref[:, cols]
            k2 = k * (1.0 + (a - 1.0) * ka_ref[:, cols])
            cum = _sum2(jnp.dot(lowc, _split2(lw), preferred_element_type=f32), w)
            yield
            cend = cum[c - 1:c, :]
            e_incl = jnp.exp(cum)
            e_inv = jnp.exp(-cum)
            e_end = jnp.exp(cend - cum)
            gam = jnp.exp(cend)
            n0, n1 = halfsums(kkp * kkp)
            rn0 = 1.0 / jnp.maximum(jnp.sqrt(n0), 1e-12)
            rn1 = 1.0 / jnp.maximum(jnp.sqrt(n1), 1e-12)
            kkn = kkp * jnp.where(m0, rn0, rn1)
            ka_ = kkn * a
            kkt_b = stack((kkn * jnp.exp(cum - lw)).astype(bf16))
            bh_b = stack((ka_ * e_inv).astype(bf16))
            bb_b = stack((ka_ * e_end).astype(bf16))
            rt_b = stack((r * e_incl).astype(bf16))
            kh_b = stack((k2 * e_inv).astype(bf16))
            kb_b = stack((k2 * e_end).astype(bf16))
            v_b = stack(v.astype(bf16))
            gram = _dnt(jnp.concatenate([kkt_b, rt_b], axis=0), jnp.concatenate([kh_b, bh_b], axis=0))
            yield
            mb = slow_bd * gram[0:c2, c2:2 * c2]
            mk_b = (slow_bd * gram[0:c2, 0:c2]).astype(bf16)
            pk_b = (low_bd * gram[c2:2 * c2, 0:c2]).astype(bf16)
            pb_b = (low_bd * gram[c2:2 * c2, c2:2 * c2]).astype(bf16)
            mkv = _d(mk_b, v_b)
            pkv = _d(pk_b, v_b)
            vk = _dtn(v_b, kb_b)
            tm = eye - lvl[0] * mb
            mb_b = mb.astype(bf16)
            for l in range(1, len(lvl)):
                tm_b = tm.astype(bf16)
                tc = _d(tm_b, mb_b * lvl_b[l])
                yield
                tm = tm - _d(tc.astype(bf16), tm_b)
                yield
            wu = _d(tm.astype(bf16), jnp.concatenate([kkt_b, mkv.astype(bf16)], axis=1))
            yield
            s = st_ref[p]
            s_b = s.astype(bf16)
            u_b = (_dnt(wu[:, 0:w].astype(bf16), s_b) + wu[:, w:2 * w]).astype(bf16)
            yr = _dnt(rt_b, s_b) + pkv
            yield
            y_s = yr - _d(pb_b, u_b)
            st_ref[p] = s * gam + vk - _dtn(u_b, bb_b)
            yield
            mean = jnp.sum(y_s, axis=-1, keepdims=True) * (1.0 / n)
            yc = jnp.where(hmask, y_s - mean, 0.0)
            var = jnp.sum(yc * yc, axis=-1, keepdims=True) * (1.0 / n)
            yn_s = yc * lax.rsqrt(var + GN_EPS)
            yn = (yn_s[0:c] + yn_s[c:c2]) * lnw_ref[:, cols] + lnb_ref[:, cols]
            b0, b1 = halfsums(r * k2 * rk_ref[:, cols])
            bonus = jnp.where(m0, b0, b1) * v
            o_ref[rows, cols] = ((yn + bonus) * gate).astype(o_ref.dtype)
            yield

        _round_robin([pair_steps(p) for p in range(npair)])
        return carry

    lax.fori_loop(0, nchunks, chunk, 0)

    @pl.when(tb == pl.num_programs(2) - 1)
    def _():
        for p in range(npair):
            so_ref[0, 2 * p] = st_ref[p, 0:n, 0:n]
            so_ref[0, 2 * p + 1] = st_ref[p, n:w, n:w]


def _rwkv(acts, hidden, weights2, layer, vecs, s0, row_off, nseq, t, dst):
    m, d = acts[0].shape
    heads = d // RW_HEAD
    pairs = heads // 2
    npair = _pick(pairs, (RW_PAIRS_PER_STEP, 4, 2, 1))
    c = min(RW_CHUNK, t)
    tblk = _pick(t, (256, 128, 64, 32))
    nblk = t // tblk
    off = row_off // tblk
    wblk = npair * 2 * RW_HEAD
    aspec = pl.BlockSpec((tblk, wblk), lambda s, h, b: (off + s * nblk + b, h))
    hspecs = [pl.BlockSpec((tblk, x.shape[1]), lambda s, h, b: (off + s * nblk + b, 0)) for x in hidden]
    wspecs = [pl.BlockSpec((None, x.shape[1], wblk), lambda s, h, b: (layer, 0, h)) for x in weights2]
    vspec = pl.BlockSpec((1, wblk), lambda s, h, b: (0, h))
    sspec = pl.BlockSpec((1, 2 * npair, RW_HEAD, RW_HEAD), lambda s, h, b: (s, h, 0, 0))
    body = functools.partial(_rwkv_body, c=c, nchunks=tblk // c, npair=npair)
    return _group_call(
        body, (nseq, pairs // npair, nblk),
        [aspec] * 3 + hspecs + wspecs + [vspec] * 7 + [sspec],
        list(acts) + list(hidden) + list(weights2) + [x.reshape(1, d) for x in vecs] + [s0],
        [aspec, sspec],
        [jax.ShapeDtypeStruct((m, d), bf16), jax.ShapeDtypeStruct((nseq, heads, RW_HEAD, RW_HEAD), f32)],
        dst,
        [pltpu.VMEM((npair, 2 * RW_HEAD, 2 * RW_HEAD), f32)] + [pltpu.VMEM((x.shape[1], wblk), bf16) for x in weights2],
        ("parallel", "parallel", "arbitrary"), "rwkv7_scan")


def _upconv_body(x_ref, wg_ref, wu_ref, c0p_ref, c0s_ref, cw_ref, cb_ref, o_ref, cnp_ref, cns_ref,
                 wb_ref, carry_ref, *, n_ptiles, tiles_per_seq, ts):
    i = pl.program_id(1)
    tf = o_ref.shape[1]

    @pl.when(i == 0)
    def _():
        wb_ref[:, 0:tf] = wg_ref[...].astype(bf16)
        wb_ref[:, tf:2 * tf] = wu_ref[...].astype(bf16)

    both = jnp.dot(x_ref[...], wb_ref[...], preferred_element_type=f32)
    hg = both[:, 0:tf]
    hu = both[:, tf:2 * tf]
    tm = hg.shape[0]
    cw = cw_ref[...]
    row = lax.broadcasted_iota(jnp.int32, (tm, tf), 0)
    r1 = pltpu.roll(hg, 1, axis=0)
    r2 = pltpu.roll(hg, 2, axis=0)

    def gated(p2, p1, g0, u0):
        hc = cb_ref[...] + (cw[0:1, :] * p2 + cw[1:2, :] * p1 + cw[2:3, :] * g0)
        act = 0.5 * hc * (1.0 + lax.erf(hc * 0.7071067811865476))
        return (act * u0).astype(o_ref.dtype)

    def finish(t, hm1, hm2):
        p1 = jnp.where(t == 0, hm1, r1)
        p2 = jnp.where(t == 0, hm2, jnp.where(t == 1, hm1, r2))
        o_ref[...] = gated(p2, p1, hg, hu)

    @pl.when(i < n_ptiles)
    def _():
        @pl.when(i % tiles_per_seq == 0)
        def _():
            carry_ref[...] = c0p_ref[0]
        prev = carry_ref[...]
        finish(row, prev[1:2, :], prev[0:1, :])
        carry_ref[...] = hg[tm - 2:tm, :]

        @pl.when(i % tiles_per_seq == tiles_per_seq - 1)
        def _():
            cnp_ref[0] = hg[tm - 2:tm, :]

    @pl.when(i >= n_ptiles)
    def _():
        c0 = c0s_ref[...]
        nsq = tm // ts
        hm1 = jnp.broadcast_to(c0[:, 1:2, :], (nsq, ts, tf)).reshape(tm, tf)
        hm2 = jnp.broadcast_to(c0[:, 0:1, :], (nsq, ts, tf)).reshape(tm, tf)
        finish(row % ts, hm1, hm2)
        cns_ref[...] = hg.reshape(nsq, ts, tf)[:, ts - 2:ts, :]


def _ffn_up_conv(xn, w_up, layer, c0p, c0s, cw, cb, mp, tp, ts):
    m, k = xn.shape
    f = w_up.shape[2] // 2
    ms = m - mp
    tm = next(t for t in (1024, 512, 256, 128, 64, 32, 16, 8) if tp % t == 0 and ms % t == 0 and t % ts == 0)
    tf = _pick(f, (256, 128))
    nf = f // tf
    n_ptiles = mp // tm
    tiles_per_seq = tp // tm
    nsq = tm // ts
    bp = c0p.shape[0]
    body = functools.partial(_upconv_body, n_ptiles=n_ptiles, tiles_per_seq=tiles_per_seq, ts=ts)
    return pl.pallas_call(
        body,
        grid=(nf, m // tm),
        in_specs=[pl.BlockSpec((tm, k), lambda j, i: (i, 0)),
                  pl.BlockSpec((None, k, tf), lambda j, i: (layer, 0, j)),
                  pl.BlockSpec((None, k, tf), lambda j, i: (layer, 0, nf + j)),
                  pl.BlockSpec((1, 2, tf), lambda j, i: (jnp.minimum(i // tiles_per_seq, bp - 1), 0, j)),
                  pl.BlockSpec((nsq, 2, tf), lambda j, i: (jnp.maximum(i - n_ptiles, 0), 0, j)),
                  pl.BlockSpec((3, tf), lambda j, i: (0, j)),
                  pl.BlockSpec((1, tf), lambda j, i: (0, j))],
        out_specs=[pl.BlockSpec((tm, tf), lambda j, i: (i, j)),
                   pl.BlockSpec((1, 2, tf), lambda j, i: (jnp.minimum(i // tiles_per_seq, bp - 1), 0, j)),
                   pl.BlockSpec((nsq, 2, tf), lambda j, i: (jnp.maximum(i - n_ptiles, 0), 0, j))],
        out_shape=[jax.ShapeDtypeStruct((m, f), bf16),
                   jax.ShapeDtypeStruct((bp, 2, f), f32),
                   jax.ShapeDtypeStruct((c0s.shape[0], 2, f), f32)],
        scratch_shapes=[pltpu.VMEM((k, 2 * tf), bf16), pltpu.VMEM((2, tf), f32)],
        compiler_params=_params(("parallel", "arbitrary")),
        name="ffn_up_conv",
    )(xn, w_up, w_up, c0p, c0s, cw, cb.reshape(1, f))


def _first(accs, tiles, rows):
    return accs[0]


def _resid(accs, tiles, rows):
    return tiles[0] + accs[0]


def _ple(accs, tiles, rows):
    return tiles[0] + jax.nn.sigmoid(accs[0]) * accs[1]


def kernel(x_prompt, x_sample, p_prompt, p_sample, state_hgrn, state_rwkv, state_shift, state_ffn_conv, norm_mix, norm_ffn, norm_ple, norm_final, hg_w_in, hg_lb_logits, hg_gnorm, hg_w_o, rw_mu, rw_w_rkv, rw_w0, rw_w1, rw_w2, rw_a0, rw_a1, rw_a2, rw_g1, rw_g2, rw_k_k, rw_k_a, rw_r_k, rw_lnx_w, rw_lnx_b, rw_w_o, ffn_w_up, ffn_conv_w, ffn_conv_b, ffn_w_down, ple_w_proj, ple_w_gate):
    bp, tp, d = x_prompt.shape
    bs, ts, _ = x_sample.shape
    depth = norm_mix.shape[0]
    mp, ms = bp * tp, bs * ts
    groups = ((0, bp, tp), (mp, bs, ts))
    f_ff = ffn_conv_b.shape[1]
    w_down_bf = ffn_w_down.astype(bf16)

    h = jnp.concatenate([x_prompt.reshape(mp, d), x_sample.reshape(ms, d)], axis=0)
    p_all = jnp.concatenate([p_prompt.reshape(depth, mp, -1), p_sample.reshape(depth, ms, -1)], axis=1).astype(bf16)
    lb_all = jnp.cumsum(jax.nn.softmax(hg_lb_logits.astype(f32), axis=0), axis=0)

    hg_states = [[], []]
    rw_states = [[], []]
    sh_states = [[], []]
    cv_states = [[], []]
    for i in range(depth):
        j = i // 2
        if i % 2 == 0:
            xn = _rmsnorm(h, norm_mix[i], bf16)
            z = _mm([(xn, hg_w_in, j)], _first, f32, name="hgrn_in")
            mix_in = jnp.zeros((mp + ms, d), bf16)
            for gi, (off, nseq, t) in enumerate(groups):
                s0 = jnp.zeros((nseq, d // HG_HEAD, HG_HEAD, HG_HEAD), f32) if gi == 0 else state_hgrn[j].astype(f32)
                mix_in, s_new = _hgrn(z, lb_all[j], hg_gnorm[j], s0, off, nseq, t, mix_in)
                hg_states[gi].append(s_new)
            h = _mm([(mix_in, hg_w_o, j)], _resid, f32, tiles=(h,), name="hgrn_out")
        else:
            xr, xk, xv, *hidden = _norm_shift_mix(h, norm_mix[i], rw_mu[j], jnp.zeros((bp, d), f32),
                                                  state_shift[j].astype(f32), (rw_w1, rw_a1, rw_g1), j, mp, tp, ts)
            h_last = jnp.concatenate([h[tp - 1:mp:tp], h[mp + ts - 1::ts]], axis=0)
            xn_last = _rmsnorm(h_last, norm_mix[i], f32)
            sh_states[0].append(xn_last[:bp])
            sh_states[1].append(xn_last[bp:])
            r = _mm([(xr, rw_w_rkv, j)], _first, f32, ncols=d, name="rwkv_r")
            k = _mm([(xk, rw_w_rkv, j)], _first, f32, ncols=d, col_off=d, name="rwkv_k")
            v = _mm([(xv, rw_w_rkv, j)], _first, f32, ncols=d, col_off=2 * d, name="rwkv_v")
            vecs = (rw_k_k[j], rw_k_a[j], rw_r_k[j].reshape(d), rw_lnx_w[j], rw_lnx_b[j], rw_w0[j], rw_a0[j])
            mix_in = jnp.zeros((mp + ms, d), bf16)
            for gi, (off, nseq, t) in enumerate(groups):
                s0 = jnp.zeros((nseq, d // RW_HEAD, RW_HEAD, RW_HEAD), f32) if gi == 0 else state_rwkv[j].astype(f32)
                mix_in, s_new = _rwkv((r, k, v), hidden, (rw_w2, rw_a2, rw_g2), j, vecs, s0, off, nseq, t, mix_in)
                rw_states[gi].append(s_new)
            h = _mm([(mix_in, rw_w_o, j)], _resid, f32, tiles=(h,), name="rwkv_out")

        xn = _rmsnorm(h, norm_ffn[i], bf16)
        act, cn_p, cn_s = _ffn_up_conv(xn, ffn_w_up, i, jnp.zeros((bp, 2, f_ff), f32),
                                       state_ffn_conv[i].astype(f32), ffn_conv_w[i], ffn_conv_b[i], mp, tp, ts)
        cv_states[0].append(cn_p)
        cv_states[1].append(cn_s)
        h = _mm([(act, w_down_bf, i)], _resid, f32, tiles=(h,), tm=512, tn=512, stationary=False, name="ffn_down")

        xn = _rmsnorm(h, norm_ple[i], bf16)
        h = _mm([(xn, ple_w_gate, i), (p_all[i], ple_w_proj, i)], _ple, f32, tiles=(h,), name="ple")

    y_prompt = _rmsnorm(h, norm_final, f32, 0, mp).reshape(bp, tp, d)
    y_sample = _rmsnorm(h, norm_final, f32, mp, ms).reshape(bs, ts, d)
    st = lambda xs: jnp.stack(xs)
    return (y_prompt, y_sample,
            st(hg_states[0]), st(rw_states[0]), st(sh_states[0]), st(cv_states[0]),
            st(hg_states[1]), st(rw_states[1]), st(sh_states[1]), st(cv_states[1]))
```

```python
import functools

import numpy as np
import jax
import jax.numpy as jnp
from jax import lax
from jax.experimental import pallas as pl
from jax.experimental.pallas import tpu as pltpu

f32 = jnp.float32
bf16 = jnp.bfloat16

RMS_EPS = 1e-6
GN_EPS = 64e-5
HG_HEAD = 128
HG_CHUNK = 128
HG_HEADS_PER_STEP = 16
RW_HEAD = 64
RW_CHUNK = 64
RW_PAIRS_PER_STEP = 16
VMEM_LIMIT = 56 * 1024 * 1024


def _pick(n, prefs):
    for p in prefs:
        if n % p == 0:
            return p
    return n


def _params(sem):
    return pltpu.CompilerParams(dimension_semantics=sem, vmem_limit_bytes=VMEM_LIMIT)


def _nn(a, b):
    return jnp.dot(a.astype(bf16), b.astype(bf16), preferred_element_type=f32)


def _nt(a, b):
    return lax.dot_general(a.astype(bf16), b.astype(bf16), (((1,), (1,)), ((), ())), preferred_element_type=f32)


def _tn(a, b):
    return lax.dot_general(a.astype(bf16), b.astype(bf16), (((0,), (0,)), ((), ())), preferred_element_type=f32)


def _d(a, b):
    return jnp.dot(a, b, preferred_element_type=f32)


def _dnt(a, b):
    return lax.dot_general(a, b, (((1,), (1,)), ((), ())), preferred_element_type=f32)


def _dtn(a, b):
    return lax.dot_general(a, b, (((0,), (0,)), ((), ())), preferred_element_type=f32)


def _split2(x):
    hi = x.astype(bf16)
    lo = (x - hi.astype(f32)).astype(bf16)
    return jnp.concatenate([hi, lo], axis=1)


def _sum2(d2, w):
    return d2[:, 0:w] + d2[:, w:2 * w]


def _round_robin(gens):
    for _ in zip(*gens):
        pass


def _group_call(body, grid, in_specs, ins, out_specs, out_shapes, dst, scratch, sem, name):
    return pl.pallas_call(
        body,
        grid=grid, in_specs=list(in_specs) + [pl.BlockSpec(memory_space=pl.ANY)],
        out_specs=out_specs, out_shape=out_shapes,
        scratch_shapes=scratch, input_output_aliases={len(ins): 0},
        compiler_params=_params(sem), name=name,
    )(*ins, dst)


def _rms_body(x_ref, g_ref, o_ref):
    x = x_ref[...]
    ms = jnp.mean(x * x, axis=-1, keepdims=True)
    o_ref[...] = (x * lax.rsqrt(ms + RMS_EPS) * g_ref[...]).astype(o_ref.dtype)


def _rmsnorm(x, g, out_dtype, row_off=0, nrows=None):
    d = x.shape[1]
    m = x.shape[0] if nrows is None else nrows
    tm = _pick(m, (256, 128, 64, 32, 16, 8))
    off = row_off // tm
    return pl.pallas_call(
        _rms_body,
        grid=(m // tm,),
        in_specs=[pl.BlockSpec((tm, d), lambda i: (off + i, 0)), pl.BlockSpec((1, d), lambda i: (0, 0))],
        out_specs=pl.BlockSpec((tm, d), lambda i: (i, 0)),
        out_shape=jax.ShapeDtypeStruct((m, d), out_dtype),
        compiler_params=_params(("parallel",)),
        name="rmsnorm",
    )(x, g.reshape(1, d))


def _mm_body(*refs, nd, nt, nr, epi, stationary):
    dots = refs[: 2 * nd]
    tiles = refs[2 * nd: 2 * nd + nt]
    rows = refs[2 * nd + nt: 2 * nd + nt + nr]
    o_ref = refs[2 * nd + nt + nr]
    if stationary:
        wbs = refs[2 * nd + nt + nr + 1:]

        @pl.when(pl.program_id(1) == 0)
        def _():
            for q in range(nd):
                wbs[q][...] = dots[2 * q + 1][...].astype(bf16)
        ws = [wb[...] for wb in wbs]
    else:
        ws = [dots[2 * q + 1][...] for q in range(nd)]
    accs = [jnp.dot(dots[2 * q][...], ws[q], preferred_element_type=f32) for q in range(nd)]
    o_ref[...] = epi(accs, [t[...] for t in tiles], [r[...] for r in rows]).astype(o_ref.dtype)


def _mm(dots, epi, out_dtype, tiles=(), rows=(), tm=1024, tn=512, ncols=None, col_off=0, stationary=True, name="mm"):
    m = dots[0][0].shape[0]
    n = dots[0][1].shape[2] if ncols is None else ncols
    tm = _pick(m, (tm, 512, 256, 128, 64, 32, 16, 8))
    tn = _pick(n, (tn, 512, 256, 128))
    assert col_off % tn == 0
    joff = col_off // tn
    if stationary:
        grid = (n // tn, m // tm)
        ij = lambda a, b: (b, a)
    else:
        grid = (m // tm, n // tn)
        ij = lambda a, b: (a, b)
    ins, specs, scratch = [], [], []
    for x, w, layer in dots:
        k = x.shape[1]
        ins += [x, w]
        specs += [pl.BlockSpec((tm, k), lambda a, b: (ij(a, b)[0], 0)),
                  pl.BlockSpec((None, k, tn), lambda a, b, _l=layer: (_l, 0, joff + ij(a, b)[1]))]
        if stationary:
            scratch.append(pltpu.VMEM((k, tn), bf16))
    for t in tiles:
        ins.append(t)
        specs.append(pl.BlockSpec((tm, tn), lambda a, b: ij(a, b)))
    for r in rows:
        ins.append(r.reshape(1, n))
        specs.append(pl.BlockSpec((1, tn), lambda a, b: (0, ij(a, b)[1])))
    body = functools.partial(_mm_body, nd=len(dots), nt=len(tiles), nr=len(rows), epi=epi, stationary=stationary)
    return pl.pallas_call(
        body,
        grid=grid,
        in_specs=specs,
        out_specs=pl.BlockSpec((tm, tn), lambda a, b: ij(a, b)),
        out_shape=jax.ShapeDtypeStruct((m, n), out_dtype),
        scratch_shapes=scratch,
        compiler_params=_params(("parallel", "arbitrary")),
        name=name,
    )(*ins)


def _hgrn_consts(c):
    idx = np.arange(c)
    mats = [(idx[:, None] >= idx[None, :])]
    masks = [np.eye(c, dtype=bool)]
    b = 1
    while b < c:
        blk = idx // (2 * b)
        second = (idx % (2 * b)) >= b
        bnd = blk * 2 * b + b - 1
        lq = (idx[None, :] > bnd[:, None]) & (idx[None, :] <= idx[:, None]) & second[:, None]
        lk = (idx[None, :] > idx[:, None]) & (idx[None, :] <= bnd[:, None]) & (~second)[:, None]
        mats.append(lq | lk)
        masks.append((blk[:, None] == blk[None, :]) & second[:, None] & (~second)[None, :])
        b *= 2
    lmat = np.concatenate(mats, axis=0).astype(np.float32)
    return jnp.asarray(lmat, dtype=bf16), jnp.asarray(np.stack(masks).astype(np.float32)), len(mats) - 1


def _hgrn_body(q_ref, f_ref, i_ref, g_ref, lb_ref, gn_ref, s0_ref, l_ref, mask_ref, dst_ref, o_ref, so_ref, st_ref, *,
               c, nl, nchunks, nh):
    del dst_ref
    tb = pl.program_id(2)
    w = HG_HEAD

    @pl.when(tb == 0)
    def _():
        for hh in range(nh):
            st_ref[hh] = s0_ref[0, hh].T

    gn = gn_ref[...]

    def chunk(ci, carry):
        r0 = pl.multiple_of(ci * c, c)
        rows = pl.ds(r0, c)
        def head_steps(hh):
            cols = slice(hh * w, (hh + 1) * w)
            lb = lb_ref[:, cols]
            q = q_ref[rows, cols]
            fg = lb + (1.0 - lb) * jax.nn.sigmoid(f_ref[rows, cols])
            gl = jnp.log(fg)
            kk = 1.0 - fg
            qs = q * jax.nn.sigmoid(q)
            v = i_ref[rows, cols]
            d = _sum2(jnp.dot(l_ref[...], _split2(gl), preferred_element_type=f32), w)
            a = mask_ref[0] * _nt(qs, kk)
            yield
            gcum = d[0:c]
            gend = gcum[c - 1:c, :]
            st = st_ref[hh]
            oi = _nt(qs * jnp.exp(gcum), st)
            st_ref[hh] = st * jnp.exp(gend) + _tn(v, kk * jnp.exp(gend - gcum))
            for l in range(nl):
                e = jnp.exp(d[(1 + l) * c:(2 + l) * c])
                a = a + mask_ref[1 + l] * _nt(qs * e, kk * e)
            yield
            o = _nn(a, v) + oi
            yield
            ms = jnp.mean(o * o, axis=-1, keepdims=True)
            on = o * lax.rsqrt(ms + RMS_EPS) * gn
            o_ref[rows, cols] = (on * jax.nn.sigmoid(g_ref[rows, cols])).astype(o_ref.dtype)
            yield

        _round_robin([head_steps(hh) for hh in range(nh)])
        return carry

    lax.fori_loop(0, nchunks, chunk, 0)

    @pl.when(tb == pl.num_programs(2) - 1)
    def _():
        for hh in range(nh):
            so_ref[0, hh] = st_ref[hh].T


def _hgrn(z, lb, gnorm, s0, row_off, nseq, t, dst):
    m = z.shape[0]
    d = z.shape[1] // 4
    heads = d // HG_HEAD
    c = min(HG_CHUNK, t)
    nh = _pick(heads, (HG_HEADS_PER_STEP, 2, 1))
    tblk = _pick(t, (256, 128, 64, 32))
    nblk = t // tblk
    off = row_off // tblk
    hgrp = heads // nh
    wblk = nh * HG_HEAD
    lmat, masks, nl = _hgrn_consts(c)

    def zspec(sec):
        return pl.BlockSpec((tblk, wblk), lambda s, h, b: (off + s * nblk + b, sec * hgrp + h))

    sspec = pl.BlockSpec((1, nh, HG_HEAD, HG_HEAD), lambda s, h, b: (s, h, 0, 0))
    body = functools.partial(_hgrn_body, c=c, nl=nl, nchunks=tblk // c, nh=nh)
    return _group_call(
        body, (nseq, hgrp, nblk),
        [zspec(0), zspec(1), zspec(2), zspec(3),
         pl.BlockSpec((1, wblk), lambda s, h, b: (0, h)),
         pl.BlockSpec((1, HG_HEAD), lambda s, h, b: (0, 0)),
         sspec,
         pl.BlockSpec(lmat.shape, lambda s, h, b: (0, 0)),
         pl.BlockSpec(masks.shape, lambda s, h, b: (0, 0, 0))],
        [z, z, z, z, lb.reshape(1, d), gnorm.reshape(1, HG_HEAD), s0, lmat, masks],
        [pl.BlockSpec((tblk, wblk), lambda s, h, b: (off + s * nblk + b, h)), sspec],
        [jax.ShapeDtypeStruct((m, d), bf16), jax.ShapeDtypeStruct((nseq, heads, HG_HEAD, HG_HEAD), f32)],
        dst, [pltpu.VMEM((nh, HG_HEAD, HG_HEAD), f32)], ("parallel", "parallel", "arbitrary"), "hgrn2_scan")


def _norm_shift_body(h_ref, halo_ref, g_ref, shp_ref, shs_ref, mu_ref, w1_ref, a1_ref, g1_ref,
                     or_ref, ok_ref, ov_ref, tw_ref, ta_ref, tg_ref, w1b_ref, a1b_ref, g1b_ref, *,
                     n_ptiles, tiles_per_seq, ts):
    i = pl.program_id(0)
    g = g_ref[...]

    @pl.when(i == 0)
    def _():
        w1b_ref[...] = w1_ref[...].astype(bf16)
        a1b_ref[...] = a1_ref[...].astype(bf16)
        g1b_ref[...] = g1_ref[...].astype(bf16)

    def norm(v):
        return v * lax.rsqrt(jnp.mean(v * v, axis=-1, keepdims=True) + RMS_EPS) * g

    x = norm(h_ref[...])
    tm, d = x.shape
    row = lax.broadcasted_iota(jnp.int32, (tm, d), 0)
    rolled = pltpu.roll(x, 1, axis=0)

    def finish(xp):
        xx = xp - x
        mix = lambda j: (x + xx * mu_ref[j:j + 1, :]).astype(bf16)
        or_ref[...] = mix(0)
        ok_ref[...] = mix(2)
        ov_ref[...] = mix(3)
        tw_ref[...] = jnp.tanh(_d(mix(1), w1b_ref[...])).astype(bf16)
        ta_ref[...] = _d(mix(4), a1b_ref[...]).astype(bf16)
        tg_ref[...] = jax.nn.sigmoid(_d(mix(5), g1b_ref[...])).astype(bf16)

    @pl.when(i < n_ptiles)
    def _():
        first = jnp.where(i % tiles_per_seq == 0, shp_ref[0], norm(halo_ref[...])[7:8, :])
        finish(jnp.where(row == 0, first, rolled))

    @pl.when(i >= n_ptiles)
    def _():
        nsq = tm // ts
        first = jnp.broadcast_to(shs_ref[...], (nsq, ts, d)).reshape(tm, d)
        finish(jnp.where(row % ts == 0, first, rolled))


def _norm_shift_mix(h, g, mu, shift_p, shift_s, low1, layer, mp, tp, ts):
    m, d = h.shape
    ranks = [x.shape[2] for x in low1]
    ms = m - mp
    bp, bs = shift_p.shape[0], shift_s.shape[0]
    tm = next(t for t in (128, 64, 32, 16, 8) if tp % t == 0 and ms % t == 0 and t % ts == 0)
    n_ptiles = mp // tm
    tiles_per_seq = tp // tm
    nsq = tm // ts
    hb = tm // 8
    body = functools.partial(_norm_shift_body, n_ptiles=n_ptiles, tiles_per_seq=tiles_per_seq, ts=ts)
    return pl.pallas_call(
        body,
        grid=(m // tm,),
        in_specs=[pl.BlockSpec((tm, d), lambda i: (i, 0)),
                  pl.BlockSpec((8, d), lambda i: (jnp.maximum(i * hb - 1, 0), 0)),
                  pl.BlockSpec((1, d), lambda i: (0, 0)),
                  pl.BlockSpec((1, 1, d), lambda i: (jnp.minimum(i // tiles_per_seq, bp - 1), 0, 0)),
                  pl.BlockSpec((nsq, 1, d), lambda i: (jnp.maximum(i - n_ptiles, 0), 0, 0)),
                  pl.BlockSpec((6, d), lambda i: (0, 0))]
                 + [pl.BlockSpec((None, d, r), lambda i: (layer, 0, 0)) for r in ranks],
        out_specs=[pl.BlockSpec((tm, d), lambda i: (i, 0))] * 3 + [pl.BlockSpec((tm, r), lambda i: (i, 0)) for r in ranks],
        out_shape=[jax.ShapeDtypeStruct((m, d), bf16)] * 3 + [jax.ShapeDtypeStruct((m, r), bf16) for r in ranks],
        scratch_shapes=[pltpu.VMEM((d, r), bf16) for r in ranks],
        compiler_params=_params(("arbitrary",)),
        name="rwkv_norm_shift_mix",
    )(h, h, g.reshape(1, d), shift_p.reshape(bp, 1, d), shift_s.reshape(bs, 1, d), mu, *low1)


def _rwkv_body(r_ref, k_ref, v_ref, tw_ref, ta_ref, tg_ref, w2_ref, a2_ref, g2_ref, kk_ref, ka_ref, rk_ref, lnw_ref,
               lnb_ref, w0_ref, a0_ref, s0_ref, dst_ref, o_ref, so_ref, st_ref, w2b_ref, a2b_ref, g2b_ref, *,
               c, nchunks, npair):
    del dst_ref
    tb = pl.program_id(2)
    n = RW_HEAD
    w = 2 * n
    c2 = 2 * c
    i32 = jnp.int32
    w2b_ref[...] = w2_ref[...].astype(bf16)
    a2b_ref[...] = a2_ref[...].astype(bf16)
    g2b_ref[...] = g2_ref[...].astype(bf16)

    @pl.when(tb == 0)
    def _():
        st_ref[...] = jnp.zeros(st_ref.shape, f32)
        for p in range(npair):
            st_ref[p, 0:n, 0:n] = s0_ref[0, 2 * p]
            st_ref[p, n:w, n:w] = s0_ref[0, 2 * p + 1]

    ri = lax.broadcasted_iota(i32, (c2, c2), 0)
    cj = lax.broadcasted_iota(i32, (c2, c2), 1)
    same = (ri // c) == (cj // c)
    low_bd = (same & (ri >= cj)).astype(f32)
    slow_bd = (same & (ri > cj)).astype(f32)
    eye = (ri == cj).astype(f32)
    lvl = []
    b = 1
    while b < c:
        lvl.append(((ri // (2 * b) == cj // (2 * b)) & (ri % (2 * b) >= b) & (cj % (2 * b) < b)).astype(f32))
        b *= 2
    lvl_b = [m.astype(bf16) for m in lvl]
    lowc = (lax.broadcasted_iota(i32, (c, c), 0) >= lax.broadcasted_iota(i32, (c, c), 1)).astype(bf16)
    m0 = lax.broadcasted_iota(i32, (c, w), 1) < n
    hmask = (lax.broadcasted_iota(i32, (c2, w), 0) // c) == (lax.broadcasted_iota(i32, (c2, w), 1) // n)

    lane0 = m0.astype(bf16)
    lane1 = 1.0 - lane0

    def stack(x_b):
        return jnp.concatenate([x_b * lane0, x_b * lane1], axis=0)

    def halfsums(x):
        s0_ = jnp.sum(jnp.where(m0, x, 0.0), axis=-1, keepdims=True)
        s1_ = jnp.sum(jnp.where(m0, 0.0, x), axis=-1, keepdims=True)
        return s0_, s1_

    def chunk(ci, carry):
        r0 = pl.multiple_of(ci * c, c)
        rows = pl.ds(r0, c)
        tw = tw_ref[rows, :]
        ta = ta_ref[rows, :]
        tg = tg_ref[rows, :]

        def pair_steps(p):
            cols = slice(p * w, (p + 1) * w)
            r = r_ref[rows, cols]
            k = k_ref[rows, cols]
            v = v_ref[rows, cols]
            w_lin = _d(tw, w2b_ref[:, cols]) + w0_ref[:, cols]
            a_lin = _d(ta, a2b_ref[:, cols]) + a0_ref[:, cols]
            gate = _d(tg, g2b_ref[:, cols])
            yield
            lw = -jnp.exp(-jax.nn.softplus(-w_lin) - 0.5)
            a = jax.nn.sigmoid(a_lin)
            kkp = k * kk_ref[:, cols]
            k2 = k * (1.0 + (a - 1.0) * ka_ref[:, cols])
            cum = _sum2(jnp.dot(lowc, _split2(lw), preferred_element_type=f32), w)
            yield
            cend = cum[c - 1:c, :]
            e_incl = jnp.exp(cum)
            e_inv = jnp.exp(-cum)
            e_end = jnp.exp(cend - cum)
            gam = jnp.exp(cend)
            n0, n1 = halfsums(kkp * kkp)
            rn0 = 1.0 / jnp.maximum(jnp.sqrt(n0), 1e-12)
            rn1 = 1.0 / jnp.maximum(jnp.sqrt(n1), 1e-12)
            kkn = kkp * jnp.where(m0, rn0, rn1)
            ka_ = kkn * a
            kkt_b = stack((kkn * jnp.exp(cum - lw)).astype(bf16))
            bh_b = stack((ka_ * e_inv).astype(bf16))
            bb_b = stack((ka_ * e_end).astype(bf16))
            rt_b = stack((r * e_incl).astype(bf16))
            kh_b = stack((k2 * e_inv).astype(bf16))
            kb_b = stack((k2 * e_end).astype(bf16))
            v_b = stack(v.astype(bf16))
            gram = _dnt(jnp.concatenate([kkt_b, rt_b], axis=0), jnp.concatenate([kh_b, bh_b], axis=0))
            yield
            mb = slow_bd * gram[0:c2, c2:2 * c2]
            mk_b = (slow_bd * gram[0:c2, 0:c2]).astype(bf16)
            pk_b = (low_bd * gram[c2:2 * c2, 0:c2]).astype(bf16)
            pb_b = (low_bd * gram[c2:2 * c2, c2:2 * c2]).astype(bf16)
            mkv = _d(mk_b, v_b)
            pkv = _d(pk_b, v_b)
            vk = _dtn(v_b, kb_b)
            tm = eye - lvl[0] * mb
            mb_b = mb.astype(bf16)
            for l in range(1, len(lvl)):
                tm_b = tm.astype(bf16)
                tc = _d(tm_b, mb_b * lvl_b[l])
                yield
                tm = tm - _d(tc.astype(bf16), tm_b)
                yield
            wu = _d(tm.astype(bf16), jnp.concatenate([kkt_b, mkv.astype(bf16)], axis=1))
            yield
            s = st_ref[p]
            s_b = s.astype(bf16)
            u_b = (_dnt(wu[:, 0:w].astype(bf16), s_b) + wu[:, w:2 * w]).astype(bf16)
            yr = _dnt(rt_b, s_b) + pkv
            yield
            y_s = yr - _d(pb_b, u_b)
            st_ref[p] = s * gam + vk - _dtn(u_b, bb_b)
            yield
            mean = jnp.sum(y_s, axis=-1, keepdims=True) * (1.0 / n)
            yc = jnp.where(hmask, y_s - mean, 0.0)
            var = jnp.sum(yc * yc, axis=-1, keepdims=True) * (1.0 / n)
            yn_s = yc * lax.rsqrt(var + GN_EPS)
            yn = (yn_s[0:c] + yn_s[c:c2]) * lnw_ref[:, cols] + lnb_ref[:, cols]
            b0, b1 = halfsums(r * k2 * rk_ref[:, cols])
            bonus = jnp.where(m0, b0, b1) * v
            o_ref[rows, cols] = ((yn + bonus) * gate).astype(o_ref.dtype)
            yield

        _round_robin([pair_steps(p) for p in range(npair)])
        return carry

    lax.fori_loop(0, nchunks, chunk, 0)

    @pl.when(tb == pl.num_programs(2) - 1)
    def _():
        for p in range(npair):
            so_ref[0, 2 * p] = st_ref[p, 0:n, 0:n]
            so_ref[0, 2 * p + 1] = st_ref[p, n:w, n:w]


def _rwkv(acts, hidden, weights2, layer, vecs, s0, row_off, nseq, t, dst):
    m, d = acts[0].shape
    heads = d // RW_HEAD
    pairs = heads // 2
    npair = _pick(pairs, (RW_PAIRS_PER_STEP, 4, 2, 1))
    c = min(RW_CHUNK, t)
    tblk = _pick(t, (256, 128, 64, 32))
    nblk = t // tblk
    off = row_off // tblk
    wblk = npair * 2 * RW_HEAD
    aspec = pl.BlockSpec((tblk, wblk), lambda s, h, b: (off + s * nblk + b, h))
    hspecs = [pl.BlockSpec((tblk, x.shape[1]), lambda s, h, b: (off + s * nblk + b, 0)) for x in hidden]
    wspecs = [pl.BlockSpec((None, x.shape[1], wblk), lambda s, h, b: (layer, 0, h)) for x in weights2]
    vspec = pl.BlockSpec((1, wblk), lambda s, h, b: (0, h))
    sspec = pl.BlockSpec((1, 2 * npair, RW_HEAD, RW_HEAD), lambda s, h, b: (s, h, 0, 0))
    body = functools.partial(_rwkv_body, c=c, nchunks=tblk // c, npair=npair)
    return _group_call(
        body, (nseq, pairs // npair, nblk),
        [aspec] * 3 + hspecs + wspecs + [vspec] * 7 + [sspec],
        list(acts) + list(hidden) + list(weights2) + [x.reshape(1, d) for x in vecs] + [s0],
        [aspec, sspec],
        [jax.ShapeDtypeStruct((m, d), bf16), jax.ShapeDtypeStruct((nseq, heads, RW_HEAD, RW_HEAD), f32)],
        dst,
        [pltpu.VMEM((npair, 2 * RW_HEAD, 2 * RW_HEAD), f32)] + [pltpu.VMEM((x.shape[1], wblk), bf16) for x in weights2],
        ("parallel", "parallel", "arbitrary"), "rwkv7_scan")


def _upconv_body(x_ref, wg_ref, wu_ref, c0p_ref, c0s_ref, cw_ref, cb_ref, o_ref, cnp_ref, cns_ref,
                 wb_ref, carry_ref, *, n_ptiles, tiles_per_seq, ts):
    i = pl.program_id(1)
    tf = o_ref.shape[1]

    @pl.when(i == 0)
    def _():
        wb_ref[:, 0:tf] = wg_ref[...].astype(bf16)
        wb_ref[:, tf:2 * tf] = wu_ref[...].astype(bf16)

    both = jnp.dot(x_ref[...], wb_ref[...], preferred_element_type=f32)
    hg = both[:, 0:tf]
    hu = both[:, tf:2 * tf]
    tm = hg.shape[0]
    cw = cw_ref[...]
    row = lax.broadcasted_iota(jnp.int32, (tm, tf), 0)
    r1 = pltpu.roll(hg, 1, axis=0)
    r2 = pltpu.roll(hg, 2, axis=0)

    def gated(p2, p1, g0, u0):
        hc = cb_ref[...] + (cw[0:1, :] * p2 + cw[1:2, :] * p1 + cw[2:3, :] * g0)
        act = 0.5 * hc * (1.0 + lax.erf(hc * 0.7071067811865476))
        return (act * u0).astype(o_ref.dtype)

    def finish(t, hm1, hm2):
        p1 = jnp.where(t == 0, hm1, r1)
        p2 = jnp.where(t == 0, hm2, jnp.where(t == 1, hm1, r2))
        o_ref[...] = gated(p2, p1, hg, hu)

    @pl.when(i < n_ptiles)
    def _():
        @pl.when(i % tiles_per_seq == 0)
        def _():
            carry_ref[...] = c0p_ref[0]
        prev = carry_ref[...]
        finish(row, prev[1:2, :], prev[0:1, :])
        carry_ref[...] = hg[tm - 2:tm, :]

        @pl.when(i % tiles_per_seq == tiles_per_seq - 1)
        def _():
            cnp_ref[0] = hg[tm - 2:tm, :]

    @pl.when(i >= n_ptiles)
    def _():
        c0 = c0s_ref[...]
        nsq = tm // ts
        hm1 = jnp.broadcast_to(c0[:, 1:2, :], (nsq, ts, tf)).reshape(tm, tf)
        hm2 = jnp.broadcast_to(c0[:, 0:1, :], (nsq, ts, tf)).reshape(tm, tf)
        finish(row % ts, hm1, hm2)
        cns_ref[...] = hg.reshape(nsq, ts, tf)[:, ts - 2:ts, :]


def _ffn_up_conv(xn, w_up, layer, c0p, c0s, cw, cb, mp, tp, ts):
    m, k = xn.shape
    f = w_up.shape[2] // 2
    ms = m - mp
    tm = next(t for t in (1024, 512, 256, 128, 64, 32, 16, 8) if tp % t == 0 and ms % t == 0 and t % ts == 0)
    tf = _pick(f, (256, 128))
    nf = f // tf
    n_ptiles = mp // tm
    tiles_per_seq = tp // tm
    nsq = tm // ts
    bp = c0p.shape[0]
    body = functools.partial(_upconv_body, n_ptiles=n_ptiles, tiles_per_seq=tiles_per_seq, ts=ts)
    return pl.pallas_call(
        body,
        grid=(nf, m // tm),
        in_specs=[pl.BlockSpec((tm, k), lambda j, i: (i, 0)),
                  pl.BlockSpec((None, k, tf), lambda j, i: (layer, 0, j)),
                  pl.BlockSpec((None, k, tf), lambda j, i: (layer, 0, nf + j)),
                  pl.BlockSpec((1, 2, tf), lambda j, i: (jnp.minimum(i // tiles_per_seq, bp - 1), 0, j)),
                  pl.BlockSpec((nsq, 2, tf), lambda j, i: (jnp.maximum(i - n_ptiles, 0), 0, j)),
                  pl.BlockSpec((3, tf), lambda j, i: (0, j)),
                  pl.BlockSpec((1, tf), lambda j, i: (0, j))],
        out_specs=[pl.BlockSpec((tm, tf), lambda j, i: (i, j)),
                   pl.BlockSpec((1, 2, tf), lambda j, i: (jnp.minimum(i // tiles_per_seq, bp - 1), 0, j)),
                   pl.BlockSpec((nsq, 2, tf), lambda j, i: (jnp.maximum(i - n_ptiles, 0), 0, j))],
        out_shape=[jax.ShapeDtypeStruct((m, f), bf16),
                   jax.ShapeDtypeStruct((bp, 2, f), f32),
                   jax.ShapeDtypeStruct((c0s.shape[0], 2, f), f32)],
        scratch_shapes=[pltpu.VMEM((k, 2 * tf), bf16), pltpu.VMEM((2, tf), f32)],
        compiler_params=_params(("parallel", "arbitrary")),
        name="ffn_up_conv",
    )(xn, w_up, w_up, c0p, c0s, cw, cb.reshape(1, f))


def _first(accs, tiles, rows):
    return accs[0]


def _resid(accs, tiles, rows):
    return tiles[0] + accs[0]


def _ple(accs, tiles, rows):
    return tiles[0] + jax.nn.sigmoid(accs[0]) * accs[1]


def kernel(x_prompt, x_sample, p_prompt, p_sample, state_hgrn, state_rwkv, state_shift, state_ffn_conv, norm_mix, norm_ffn, norm_ple, norm_final, hg_w_in, hg_lb_logits, hg_gnorm, hg_w_o, rw_mu, rw_w_rkv, rw_w0, rw_w1, rw_w2, rw_a0, rw_a1, rw_a2, rw_g1, rw_g2, rw_k_k, rw_k_a, rw_r_k, rw_lnx_w, rw_lnx_b, rw_w_o, ffn_w_up, ffn_conv_w, ffn_conv_b, ffn_w_down, ple_w_proj, ple_w_gate):
    bp, tp, d = x_prompt.shape
    bs, ts, _ = x_sample.shape
    depth = norm_mix.shape[0]
    mp, ms = bp * tp, bs * ts
    groups = ((0, bp, tp), (mp, bs, ts))
    f_ff = ffn_conv_b.shape[1]
    w_down_bf = ffn_w_down.astype(bf16)

    h = jnp.concatenate([x_prompt.reshape(mp, d), x_sample.reshape(ms, d)], axis=0)
    p_all = jnp.concatenate([p_prompt.reshape(depth, mp, -1), p_sample.reshape(depth, ms, -1)], axis=1).astype(bf16)
    lb_all = jnp.cumsum(jax.nn.softmax(hg_lb_logits.astype(f32), axis=0), axis=0)

    hg_states = [[], []]
    rw_states = [[], []]
    sh_states = [[], []]
    cv_states = [[], []]
    for i in range(depth):
        j = i // 2
        if i % 2 == 0:
            xn = _rmsnorm(h, norm_mix[i], bf16)
            z = _mm([(xn, hg_w_in, j)], _first, f32, name="hgrn_in")
            mix_in = jnp.zeros((mp + ms, d), bf16)
            for gi, (off, nseq, t) in enumerate(groups):
                s0 = jnp.zeros((nseq, d // HG_HEAD, HG_HEAD, HG_HEAD), f32) if gi == 0 else state_hgrn[j].astype(f32)
                mix_in, s_new = _hgrn(z, lb_all[j], hg_gnorm[j], s0, off, nseq, t, mix_in)
                hg_states[gi].append(s_new)
            h = _mm([(mix_in, hg_w_o, j)], _resid, f32, tiles=(h,), name="hgrn_out")
        else:
            xr, xk, xv, *hidden = _norm_shift_mix(h, norm_mix[i], rw_mu[j], jnp.zeros((bp, d), f32),
                                                  state_shift[j].astype(f32), (rw_w1, rw_a1, rw_g1), j, mp, tp, ts)
            h_last = jnp.concatenate([h[tp - 1:mp:tp], h[mp + ts - 1::ts]], axis=0)
            xn_last = _rmsnorm(h_last, norm_mix[i], f32)
            sh_states[0].append(xn_last[:bp])
            sh_states[1].append(xn_last[bp:])
            r = _mm([(xr, rw_w_rkv, j)], _first, f32, ncols=d, name="rwkv_r")
            k = _mm([(xk, rw_w_rkv, j)], _first, f32, ncols=d, col_off=d, name="rwkv_k")
            v = _mm([(xv, rw_w_rkv, j)], _first, f32, ncols=d, col_off=2 * d, name="rwkv_v")
            vecs = (rw_k_k[j], rw_k_a[j], rw_r_k[j].reshape(d), rw_lnx_w[j], rw_lnx_b[j], rw_w0[j], rw_a0[j])
            mix_in = jnp.zeros((mp + ms, d), bf16)
            for gi, (off, nseq, t) in enumerate(groups):
                s0 = jnp.zeros((nseq, d // RW_HEAD, RW_HEAD, RW_HEAD), f32) if gi == 0 else state_rwkv[j].astype(f32)
                mix_in, s_new = _rwkv((r, k, v), hidden, (rw_w2, rw_a2, rw_g2), j, vecs, s0, off, nseq, t, mix_in)
                rw_states[gi].append(s_new)
            h = _mm([(mix_in, rw_w_o, j)], _resid, f32, tiles=(h,), name="rwkv_out")

        xn = _rmsnorm(h, norm_ffn[i], bf16)
        act, cn_p, cn_s = _ffn_up_conv(xn, ffn_w_up, i, jnp.zeros((bp, 2, f_ff), f32),
                                       state_ffn_conv[i].astype(f32), ffn_conv_w[i], ffn_conv_b[i], mp, tp, ts)
        cv_states[0].append(cn_p)
        cv_states[1].append(cn_s)
        h = _mm([(act, w_down_bf, i)], _resid, f32, tiles=(h,), tm=512, tn=512, stationary=False, name="ffn_down")

        xn = _rmsnorm(h, norm_ple[i], bf16)
        h = _mm([(xn, ple_w_gate, i), (p_all[i], ple_w_proj, i)], _ple, f32, tiles=(h,), name="ple")

    y_prompt = _rmsnorm(h, norm_final, f32, 0, mp).reshape(bp, tp, d)
    y_sample = _rmsnorm(h, norm_final, f32, mp, ms).reshape(bs, ts, d)
    st = lambda xs: jnp.stack(xs)
    return (y_prompt, y_sample,
            st(hg_states[0]), st(rw_states[0]), st(sh_states[0]), st(cv_states[0]),
            st(hg_states[1]), st(rw_states[1]), st(sh_states[1]), st(cv_states[1]))
```

```python
import functools

import numpy as np
import jax
import jax.numpy as jnp
from jax import lax
from jax.experimental import pallas as pl
from jax.experimental.pallas import tpu as pltpu

f32 = jnp.float32
bf16 = jnp.bfloat16

RMS_EPS = 1e-6
GN_EPS = 64e-5
HG_HEAD = 128
HG_CHUNK = 128
HG_HEADS_PER_STEP = 16
RW_HEAD = 64
RW_CHUNK = 64
RW_PAIRS_PER_STEP = 16
VMEM_LIMIT = 56 * 1024 * 1024


def _pick(n, prefs):
    for p in prefs:
        if n % p == 0:
            return p
    return n


def _params(sem):
    return pltpu.CompilerParams(dimension_semantics=sem, vmem_limit_bytes=VMEM_LIMIT)


def _nn(a, b):
    return jnp.dot(a.astype(bf16), b.astype(bf16), preferred_element_type=f32)


def _nt(a, b):
    return lax.dot_general(a.astype(bf16), b.astype(bf16), (((1,), (1,)), ((), ())), preferred_element_type=f32)


def _tn(a, b):
    return lax.dot_general(a.astype(bf16), b.astype(bf16), (((0,), (0,)), ((), ())), preferred_element_type=f32)


def _d(a, b):
    return jnp.dot(a, b, preferred_element_type=f32)


def _dnt(a, b):
    return lax.dot_general(a, b, (((1,), (1,)), ((), ())), preferred_element_type=f32)


def _dtn(a, b):
    return lax.dot_general(a, b, (((0,), (0,)), ((), ())), preferred_element_type=f32)


def _split2(x):
    hi = x.astype(bf16)
    lo = (x - hi.astype(f32)).astype(bf16)
    return jnp.concatenate([hi, lo], axis=1)


def _sum2(d2, w):
    return d2[:, 0:w] + d2[:, w:2 * w]


def _round_robin(gens):
    for _ in zip(*gens):
        pass


def _group_call(body, grid, in_specs, ins, out_specs, out_shapes, dst, scratch, sem, name):
    return pl.pallas_call(
        body,
        grid=grid, in_specs=list(in_specs) + [pl.BlockSpec(memory_space=pl.ANY)],
        out_specs=out_specs, out_shape=out_shapes,
        scratch_shapes=scratch, input_output_aliases={len(ins): 0},
        compiler_params=_params(sem), name=name,
    )(*ins, dst)


def _rms_body(x_ref, g_ref, o_ref):
    x = x_ref[...]
    ms = jnp.mean(x * x, axis=-1, keepdims=True)
    o_ref[...] = (x * lax.rsqrt(ms + RMS_EPS) * g_ref[...]).astype(o_ref.dtype)


def _rmsnorm(x, g, out_dtype, row_off=0, nrows=None):
    d = x.shape[1]
    m = x.shape[0] if nrows is None else nrows
    tm = _pick(m, (256, 128, 64, 32, 16, 8))
    off = row_off // tm
    return pl.pallas_call(
        _rms_body,
        grid=(m // tm,),
        in_specs=[pl.BlockSpec((tm, d), lambda i: (off + i, 0)), pl.BlockSpec((1, d), lambda i: (0, 0))],
        out_specs=pl.BlockSpec((tm, d), lambda i: (i, 0)),
        out_shape=jax.ShapeDtypeStruct((m, d), out_dtype),
        compiler_params=_params(("parallel",)),
        name="rmsnorm",
    )(x, g.reshape(1, d))


def _mm_body(*refs, nd, nt, nr, epi, stationary):
    dots = refs[: 2 * nd]
    tiles = refs[2 * nd: 2 * nd + nt]
    rows = refs[2 * nd + nt: 2 * nd + nt + nr]
    o_ref = refs[2 * nd + nt + nr]
    if stationary:
        wbs = refs[2 * nd + nt + nr + 1:]

        @pl.when(pl.program_id(1) == 0)
        def _():
            for q in range(nd):
                wbs[q][...] = dots[2 * q + 1][...].astype(bf16)
        ws = [wb[...] for wb in wbs]
    else:
        ws = [dots[2 * q + 1][...] for q in range(nd)]
    accs = [jnp.dot(dots[2 * q][...], ws[q], preferred_element_type=f32) for q in range(nd)]
    o_ref[...] = epi(accs, [t[...] for t in tiles], [r[...] for r in rows]).astype(o_ref.dtype)


def _mm(dots, epi, out_dtype, tiles=(), rows=(), tm=1024, tn=512, ncols=None, col_off=0, stationary=True, name="mm"):
    m = dots[0][0].shape[0]
    n = dots[0][1].shape[2] if ncols is None else ncols
    tm = _pick(m, (tm, 512, 256, 128, 64, 32, 16, 8))
    tn = _pick(n, (tn, 512, 256, 128))
    assert col_off % tn == 0
    joff = col_off // tn
    if stationary:
        grid = (n // tn, m // tm)
        ij = lambda a, b: (b, a)
    else:
        grid = (m // tm, n // tn)
        ij = lambda a, b: (a, b)
    ins, specs, scratch = [], [], []
    for x, w, layer in dots:
        k = x.shape[1]
        ins += [x, w]
        specs += [pl.BlockSpec((tm, k), lambda a, b: (ij(a, b)[0], 0)),
                  pl.BlockSpec((None, k, tn), lambda a, b, _l=layer: (_l, 0, joff + ij(a, b)[1]))]
        if stationary:
            scratch.append(pltpu.VMEM((k, tn), bf16))
    for t in tiles:
        ins.append(t)
        specs.append(pl.BlockSpec((tm, tn), lambda a, b: ij(a, b)))
    for r in rows:
        ins.append(r.reshape(1, n))
        specs.append(pl.BlockSpec((1, tn), lambda a, b: (0, ij(a, b)[1])))
    body = functools.partial(_mm_body, nd=len(dots), nt=len(tiles), nr=len(rows), epi=epi, stationary=stationary)
    return pl.pallas_call(
        body,
        grid=grid,
        in_specs=specs,
        out_specs=pl.BlockSpec((tm, tn), lambda a, b: ij(a, b)),
        out_shape=jax.ShapeDtypeStruct((m, n), out_dtype),
        scratch_shapes=scratch,
        compiler_params=_params(("parallel", "arbitrary")),
        name=name,
    )(*ins)


def _hgrn_consts(c):
    idx = np.arange(c)
    mats = [(idx[:, None] >= idx[None, :])]
    masks = [np.eye(c, dtype=bool)]
    b = 1
    while b < c:
        blk = idx // (2 * b)
        second = (idx % (2 * b)) >= b
        bnd = blk * 2 * b + b - 1
        lq = (idx[None, :] > bnd[:, None]) & (idx[None, :] <= idx[:, None]) & second[:, None]
        lk = (idx[None, :] > idx[:, None]) & (idx[None, :] <= bnd[:, None]) & (~second)[:, None]
        mats.append(lq | lk)
        masks.append((blk[:, None] == blk[None, :]) & second[:, None] & (~second)[None, :])
        b *= 2
    lmat = np.concatenate(mats, axis=0).astype(np.float32)
    return jnp.asarray(lmat, dtype=bf16), jnp.asarray(np.stack(masks).astype(np.float32)), len(mats) - 1


def _hgrn_body(q_ref, f_ref, i_ref, g_ref, lb_ref, gn_ref, s0_ref, l_ref, mask_ref, dst_ref, o_ref, so_ref, st_ref, *,
               c, nl, nchunks, nh):
    del dst_ref
    tb = pl.program_id(2)
    w = HG_HEAD

    @pl.when(tb == 0)
    def _():
        for hh in range(nh):
            st_ref[hh] = s0_ref[0, hh].T

    gn = gn_ref[...]

    def chunk(ci, carry):
        r0 = pl.multiple_of(ci * c, c)
        rows = pl.ds(r0, c)
        def head_steps(hh):
            cols = slice(hh * w, (hh + 1) * w)
            lb = lb_ref[:, cols]
            q = q_ref[rows, cols]
            fg = lb + (1.0 - lb) * jax.nn.sigmoid(f_ref[rows, cols])
            gl = jnp.log(fg)
            kk = 1.0 - fg
            qs = q * jax.nn.sigmoid(q)
            v = i_ref[rows, cols]
            d = _sum2(jnp.dot(l_ref[...], _split2(gl), preferred_element_type=f32), w)
            a = mask_ref[0] * _nt(qs, kk)
            yield
            gcum = d[0:c]
            gend = gcum[c - 1:c, :]
            st = st_ref[hh]
            oi = _nt(qs * jnp.exp(gcum), st)
            st_ref[hh] = st * jnp.exp(gend) + _tn(v, kk * jnp.exp(gend - gcum))
            for l in range(nl):
                e = jnp.exp(d[(1 + l) * c:(2 + l) * c])
                a = a + mask_ref[1 + l] * _nt(qs * e, kk * e)
            yield
            o = _nn(a, v) + oi
            yield
            ms = jnp.mean(o * o, axis=-1, keepdims=True)
            on = o * lax.rsqrt(ms + RMS_EPS) * gn
            o_ref[rows, cols] = (on * jax.nn.sigmoid(g_ref[rows, cols])).astype(o_ref.dtype)
            yield

        _round_robin([head_steps(hh) for hh in range(nh)])
        return carry

    lax.fori_loop(0, nchunks, chunk, 0)

    @pl.when(tb == pl.num_programs(2) - 1)
    def _():
        for hh in range(nh):
            so_ref[0, hh] = st_ref[hh].T


def _hgrn(z, lb, gnorm, s0, row_off, nseq, t, dst):
    m = z.shape[0]
    d = z.shape[1] // 4
    heads = d // HG_HEAD
    c = min(HG_CHUNK, t)
    nh = _pick(heads, (HG_HEADS_PER_STEP, 2, 1))
    tblk = _pick(t, (256, 128, 64, 32))
    nblk = t // tblk
    off = row_off // tblk
    hgrp = heads // nh
    wblk = nh * HG_HEAD
    lmat, masks, nl = _hgrn_consts(c)

    def zspec(sec):
        return pl.BlockSpec((tblk, wblk), lambda s, h, b: (off + s * nblk + b, sec * hgrp + h))

    sspec = pl.BlockSpec((1, nh, HG_HEAD, HG_HEAD), lambda s, h, b: (s, h, 0, 0))
    body = functools.partial(_hgrn_body, c=c, nl=nl, nchunks=tblk // c, nh=nh)
    return _group_call(
        body, (nseq, hgrp, nblk),
        [zspec(0), zspec(1), zspec(2), zspec(3),
         pl.BlockSpec((1, wblk), lambda s, h, b: (0, h)),
         pl.BlockSpec((1, HG_HEAD), lambda s, h, b: (0, 0)),
         sspec,
         pl.BlockSpec(lmat.shape, lambda s, h, b: (0, 0)),
         pl.BlockSpec(masks.shape, lambda s, h, b: (0, 0, 0))],
        [z, z, z, z, lb.reshape(1, d), gnorm.reshape(1, HG_HEAD), s0, lmat, masks],
        [pl.BlockSpec((tblk, wblk), lambda s, h, b: (off + s * nblk + b, h)), sspec],
        [jax.ShapeDtypeStruct((m, d), bf16), jax.ShapeDtypeStruct((nseq, heads, HG_HEAD, HG_HEAD), f32)],
        dst, [pltpu.VMEM((nh, HG_HEAD, HG_HEAD), f32)], ("parallel", "parallel", "arbitrary"), "hgrn2_scan")


def _norm_shift_body(h_ref, halo_ref, g_ref, shp_ref, shs_ref, mu_ref, w1_ref, a1_ref, g1_ref,
                     or_ref, ok_ref, ov_ref, tw_ref, ta_ref, tg_ref, w1b_ref, a1b_ref, g1b_ref, *,
                     n_ptiles, tiles_per_seq, ts):
    i = pl.program_id(0)
    g = g_ref[...]

    @pl.when(i == 0)
    def _():
        w1b_ref[...] = w1_ref[...].astype(bf16)
        a1b_ref[...] = a1_ref[...].astype(bf16)
        g1b_ref[...] = g1_ref[...].astype(bf16)

    def norm(v):
        return v * lax.rsqrt(jnp.mean(v * v, axis=-1, keepdims=True) + RMS_EPS) * g

    x = norm(h_ref[...])
    tm, d = x.shape
    row = lax.broadcasted_iota(jnp.int32, (tm, d), 0)
    rolled = pltpu.roll(x, 1, axis=0)

    def finish(xp):
        xx = xp - x
        mix = lambda j: (x + xx * mu_ref[j:j + 1, :]).astype(bf16)
        or_ref[...] = mix(0)
        ok_ref[...] = mix(2)
        ov_ref[...] = mix(3)
        tw_ref[...] = jnp.tanh(_d(mix(1), w1b_ref[...])).astype(bf16)
        ta_ref[...] = _d(mix(4), a1b_ref[...]).astype(bf16)
        tg_ref[...] = jax.nn.sigmoid(_d(mix(5), g1b_ref[...])).astype(bf16)

    @pl.when(i < n_ptiles)
    def _():
        first = jnp.where(i % tiles_per_seq == 0, shp_ref[0], norm(halo_ref[...])[7:8, :])
        finish(jnp.where(row == 0, first, rolled))

    @pl.when(i >= n_ptiles)
    def _():
        nsq = tm // ts
        first = jnp.broadcast_to(shs_ref[...], (nsq, ts, d)).reshape(tm, d)
        finish(jnp.where(row % ts == 0, first, rolled))


def _norm_shift_mix(h, g, mu, shift_p, shift_s, low1, layer, mp, tp, ts):
    m, d = h.shape
    ranks = [x.shape[2] for x in low1]
    ms = m - mp
    bp, bs = shift_p.shape[0], shift_s.shape[0]
    tm = next(t for t in (128, 64, 32, 16, 8) if tp % t == 0 and ms % t == 0 and t % ts == 0)
    n_ptiles = mp // tm
    tiles_per_seq = tp // tm
    nsq = tm // ts
    hb = tm // 8
    body = functools.partial(_norm_shift_body, n_ptiles=n_ptiles, tiles_per_seq=tiles_per_seq, ts=ts)
    return pl.pallas_call(
        body,
        grid=(m // tm,),
        in_specs=[pl.BlockSpec((tm, d), lambda i: (i, 0)),
                  pl.BlockSpec((8, d), lambda i: (jnp.maximum(i * hb - 1, 0), 0)),
                  pl.BlockSpec((1, d), lambda i: (0, 0)),
                  pl.BlockSpec((1, 1, d), lambda i: (jnp.minimum(i // tiles_per_seq, bp - 1), 0, 0)),
                  pl.BlockSpec((nsq, 1, d), lambda i: (jnp.maximum(i - n_ptiles, 0), 0, 0)),
                  pl.BlockSpec((6, d), lambda i: (0, 0))]
                 + [pl.BlockSpec((None, d, r), lambda i: (layer, 0, 0)) for r in ranks],
        out_specs=[pl.BlockSpec((tm, d), lambda i: (i, 0))] * 3 + [pl.BlockSpec((tm, r), lambda i: (i, 0)) for r in ranks],
        out_shape=[jax.ShapeDtypeStruct((m, d), bf16)] * 3 + [jax.ShapeDtypeStruct((m, r), bf16) for r in ranks],
        scratch_shapes=[pltpu.VMEM((d, r), bf16) for r in ranks],
        compiler_params=_params(("arbitrary",)),
        name="rwkv_norm_shift_mix",
    )(h, h, g.reshape(1, d), shift_p.reshape(bp, 1, d), shift_s.reshape(bs, 1, d), mu, *low1)


def _rwkv_body(r_ref, k_ref, v_ref, tw_ref, ta_ref, tg_ref, w2_ref, a2_ref, g2_ref, kk_ref, ka_ref, rk_ref, lnw_ref,
               lnb_ref, w0_ref, a0_ref, s0_ref, dst_ref, o_ref, so_ref, st_ref, w2b_ref, a2b_ref, g2b_ref, *,
               c, nchunks, npair):
    del dst_ref
    tb = pl.program_id(2)
    n = RW_HEAD
    w = 2 * n
    c2 = 2 * c
    i32 = jnp.int32

    @pl.when((pl.program_id(1) == 0) & (tb == 0))
    def _():
        w2b_ref[...] = w2_ref[...].astype(bf16)
        a2b_ref[...] = a2_ref[...].astype(bf16)
        g2b_ref[...] = g2_ref[...].astype(bf16)

    @pl.when(tb == 0)
    def _():
        st_ref[...] = jnp.zeros(st_ref.shape, f32)
        for p in range(npair):
            st_ref[p, 0:n, 0:n] = s0_ref[0, 2 * p]
            st_ref[p, n:w, n:w] = s0_ref[0, 2 * p + 1]

    ri = lax.broadcasted_iota(i32, (c2, c2), 0)
    cj = lax.broadcasted_iota(i32, (c2, c2), 1)
    same = (ri // c) == (cj // c)
    low_bd = (same & (ri >= cj)).astype(f32)
    slow_bd = (same & (ri > cj)).astype(f32)
    eye = (ri == cj).astype(f32)
    lvl = []
    b = 1
    while b < c:
        lvl.append(((ri // (2 * b) == cj // (2 * b)) & (ri % (2 * b) >= b) & (cj % (2 * b) < b)).astype(f32))
        b *= 2
    lvl_b = [m.astype(bf16) for m in lvl]
    lowc = (lax.broadcasted_iota(i32, (c, c), 0) >= lax.broadcasted_iota(i32, (c, c), 1)).astype(bf16)
    m0 = lax.broadcasted_iota(i32, (c, w), 1) < n
    hmask = (lax.broadcasted_iota(i32, (c2, w), 0) // c) == (lax.broadcasted_iota(i32, (c2, w), 1) // n)

    lane0 = m0.astype(bf16)
    lane1 = 1.0 - lane0

    def stack(x_b):
        return jnp.concatenate([x_b * lane0, x_b * lane1], axis=0)

    def halfsums(x):
        s0_ = jnp.sum(jnp.where(m0, x, 0.0), axis=-1, keepdims=True)
        s1_ = jnp.sum(jnp.where(m0, 0.0, x), axis=-1, keepdims=True)
        return s0_, s1_

    def chunk(ci, carry):
        r0 = pl.multiple_of(ci * c, c)
        rows = pl.ds(r0, c)
        tw = tw_ref[rows, :]
        ta = ta_ref[rows, :]
        tg = tg_ref[rows, :]

        def pair_steps(p):
            cols = slice(p * w, (p + 1) * w)
            r = r_ref[rows, cols]
            k = k_ref[rows, cols]
            v = v_ref[rows, cols]
            w_lin = _d(tw, w2b_ref[:, cols]) + w0_ref[:, cols]
            a_lin = _d(ta, a2b_ref[:, cols]) + a0_ref[:, cols]
            gate = _d(tg, g2b_ref[:, cols])
            yield
            lw = -jnp.exp(-jax.nn.softplus(-w_lin) - 0.5)
            a = jax.nn.sigmoid(a_lin)
            kkp = k * kk_ref[:, cols]
            k2 = k * (1.0 + (a - 1.0) * ka_ref[:, cols])
            cum = _sum2(jnp.dot(lowc, _split2(lw), preferred_element_type=f32), w)
            yield
            cend = cum[c - 1:c, :]
            e_incl = jnp.exp(cum)
            e_inv = jnp.exp(-cum)
            e_end = jnp.exp(cend - cum)
            gam = jnp.exp(cend)
            n0, n1 = halfsums(kkp * kkp)
            rn0 = 1.0 / jnp.maximum(jnp.sqrt(n0), 1e-12)
            rn1 = 1.0 / jnp.maximum(jnp.sqrt(n1), 1e-12)
            kkn = kkp * jnp.where(m0, rn0, rn1)
            ka_ = kkn * a
            kkt_b = stack((kkn * jnp.exp(cum - lw)).astype(bf16))
            bh_b = stack((ka_ * e_inv).astype(bf16))
            bb_b = stack((ka_ * e_end).astype(bf16))
            rt_b = stack((r * e_incl).astype(bf16))
            kh_b = stack((k2 * e_inv).astype(bf16))
            kb_b = stack((k2 * e_end).astype(bf16))
            v_b = stack(v.astype(bf16))
            gram = _dnt(jnp.concatenate([kkt_b, rt_b], axis=0), jnp.concatenate([kh_b, bh_b], axis=0))
            yield
            mb = slow_bd * gram[0:c2, c2:2 * c2]
            mk_b = (slow_bd * gram[0:c2, 0:c2]).astype(bf16)
            pk_b = (low_bd * gram[c2:2 * c2, 0:c2]).astype(bf16)
            pb_b = (low_bd * gram[c2:2 * c2, c2:2 * c2]).astype(bf16)
            mkv = _d(mk_b, v_b)
            pkv = _d(pk_b, v_b)
            vk = _dtn(v_b, kb_b)
            tm = eye - lvl[0] * mb
            mb_b = mb.astype(bf16)
            for l in range(1, len(lvl)):
                tm_b = tm.astype(bf16)
                tc = _d(tm_b, mb_b * lvl_b[l])
                yield
                tm = tm - _d(tc.astype(bf16), tm_b)
                yield
            wu = _d(tm.astype(bf16), jnp.concatenate([kkt_b, mkv.astype(bf16)], axis=1))
            yield
            s = st_ref[p]
            s_b = s.astype(bf16)
            u_b = (_dnt(wu[:, 0:w].astype(bf16), s_b) + wu[:, w:2 * w]).astype(bf16)
            yr = _dnt(rt_b, s_b) + pkv
            yield
            y_s = yr - _d(pb_b, u_b)
            st_ref[p] = s * gam + vk - _dtn(u_b, bb_b)
            yield
            mean = jnp.sum(y_s, axis=-1, keepdims=True) * (1.0 / n)
            yc = jnp.where(hmask, y_s - mean, 0.0)
            var = jnp.sum(yc * yc, axis=-1, keepdims=True) * (1.0 / n)
            yn_s = yc * lax.rsqrt(var + GN_EPS)
            yn = (yn_s[0:c] + yn_s[c:c2]) * lnw_ref[:, cols] + lnb_ref[:, cols]
            b0, b1 = halfsums(r * k2 * rk_ref[:, cols])
            bonus = jnp.where(m0, b0, b1) * v
            o_ref[rows, cols] = ((yn + bonus) * gate).astype(o_ref.dtype)
            yield

        _round_robin([pair_steps(p) for p in range(npair)])
        return carry

    lax.fori_loop(0, nchunks, chunk, 0)

    @pl.when(tb == pl.num_programs(2) - 1)
    def _():
        for p in range(npair):
            so_ref[0, 2 * p] = st_ref[p, 0:n, 0:n]
            so_ref[0, 2 * p + 1] = st_ref[p, n:w, n:w]


def _rwkv(acts, hidden, weights2, layer, vecs, s0, row_off, nseq, t, dst):
    m, d = acts[0].shape
    heads = d // RW_HEAD
    pairs = heads // 2
    npair = _pick(pairs, (RW_PAIRS_PER_STEP, 4, 2, 1))
    c = min(RW_CHUNK, t)
    tblk = _pick(t, (256, 128, 64, 32))
    nblk = t // tblk
    off = row_off // tblk
    wblk = npair * 2 * RW_HEAD
    aspec = pl.BlockSpec((tblk, wblk), lambda h, s, b: (off + s * nblk + b, h))
    hspecs = [pl.BlockSpec((tblk, x.shape[1]), lambda h, s, b: (off + s * nblk + b, 0)) for x in hidden]
    wspecs = [pl.BlockSpec((None, x.shape[1], wblk), lambda h, s, b: (layer, 0, h)) for x in weights2]
    vspec = pl.BlockSpec((1, wblk), lambda h, s, b: (0, h))
    sspec = pl.BlockSpec((1, 2 * npair, RW_HEAD, RW_HEAD), lambda h, s, b: (s, h, 0, 0))
    body = functools.partial(_rwkv_body, c=c, nchunks=tblk // c, npair=npair)
    return _group_call(
        body, (pairs // npair, nseq, nblk),
        [aspec] * 3 + hspecs + wspecs + [vspec] * 7 + [sspec],
        list(acts) + list(hidden) + list(weights2) + [x.reshape(1, d) for x in vecs] + [s0],
        [aspec, sspec],
        [jax.ShapeDtypeStruct((m, d), bf16), jax.ShapeDtypeStruct((nseq, heads, RW_HEAD, RW_HEAD), f32)],
        dst,
        [pltpu.VMEM((npair, 2 * RW_HEAD, 2 * RW_HEAD), f32)] + [pltpu.VMEM((x.shape[1], wblk), bf16) for x in weights2],
        ("parallel", "arbitrary", "arbitrary"), "rwkv7_scan")


def _upconv_body(x_ref, wg_ref, wu_ref, c0p_ref, c0s_ref, cw_ref, cb_ref, o_ref, cnp_ref, cns_ref,
                 wb_ref, carry_ref, *, n_ptiles, tiles_per_seq, ts):
    i = pl.program_id(1)
    tf = o_ref.shape[1]

    @pl.when(i == 0)
    def _():
        wb_ref[:, 0:tf] = wg_ref[...].astype(bf16)
        wb_ref[:, tf:2 * tf] = wu_ref[...].astype(bf16)

    both = jnp.dot(x_ref[...], wb_ref[...], preferred_element_type=f32)
    hg = both[:, 0:tf]
    hu = both[:, tf:2 * tf]
    tm = hg.shape[0]
    cw = cw_ref[...]
    row = lax.broadcasted_iota(jnp.int32, (tm, tf), 0)
    r1 = pltpu.roll(hg, 1, axis=0)
    r2 = pltpu.roll(hg, 2, axis=0)

    def gated(p2, p1, g0, u0):
        hc = cb_ref[...] + (cw[0:1, :] * p2 + cw[1:2, :] * p1 + cw[2:3, :] * g0)
        act = 0.5 * hc * (1.0 + lax.erf(hc * 0.7071067811865476))
        return (act * u0).astype(o_ref.dtype)

    def finish(t, hm1, hm2):
        p1 = jnp.where(t == 0, hm1, r1)
        p2 = jnp.where(t == 0, hm2, jnp.where(t == 1, hm1, r2))
        o_ref[...] = gated(p2, p1, hg, hu)

    @pl.when(i < n_ptiles)
    def _():
        @pl.when(i % tiles_per_seq == 0)
        def _():
            carry_ref[...] = c0p_ref[0]
        prev = carry_ref[...]
        finish(row, prev[1:2, :], prev[0:1, :])
        carry_ref[...] = hg[tm - 2:tm, :]

        @pl.when(i % tiles_per_seq == tiles_per_seq - 1)
        def _():
            cnp_ref[0] = hg[tm - 2:tm, :]

    @pl.when(i >= n_ptiles)
    def _():
        c0 = c0s_ref[...]
        nsq = tm // ts
        hm1 = jnp.broadcast_to(c0[:, 1:2, :], (nsq, ts, tf)).reshape(tm, tf)
        hm2 = jnp.broadcast_to(c0[:, 0:1, :], (nsq, ts, tf)).reshape(tm, tf)
        finish(row % ts, hm1, hm2)
        cns_ref[...] = hg.reshape(nsq, ts, tf)[:, ts - 2:ts, :]


def _ffn_up_conv(xn, w_up, layer, c0p, c0s, cw, cb, mp, tp, ts):
    m, k = xn.shape
    f = w_up.shape[2] // 2
    ms = m - mp
    tm = next(t for t in (1024, 512, 256, 128, 64, 32, 16, 8) if tp % t == 0 and ms % t == 0 and t % ts == 0)
    tf = _pick(f, (256, 128))
    nf = f // tf
    n_ptiles = mp // tm
    tiles_per_seq = tp // tm
    nsq = tm // ts
    bp = c0p.shape[0]
    body = functools.partial(_upconv_body, n_ptiles=n_ptiles, tiles_per_seq=tiles_per_seq, ts=ts)
    return pl.pallas_call(
        body,
        grid=(nf, m // tm),
        in_specs=[pl.BlockSpec((tm, k), lambda j, i: (i, 0)),
                  pl.BlockSpec((None, k, tf), lambda j, i: (layer, 0, j)),
                  pl.BlockSpec((None, k, tf), lambda j, i: (layer, 0, nf + j)),
                  pl.BlockSpec((1, 2, tf), lambda j, i: (jnp.minimum(i // tiles_per_seq, bp - 1), 0, j)),
                  pl.BlockSpec((nsq, 2, tf), lambda j, i: (jnp.maximum(i - n_ptiles, 0), 0, j)),
                  pl.BlockSpec((3, tf), lambda j, i: (0, j)),
                  pl.BlockSpec((1, tf), lambda j, i: (0, j))],
        out_specs=[pl.BlockSpec((tm, tf), lambda j, i: (i, j)),
                   pl.BlockSpec((1, 2, tf), lambda j, i: (jnp.minimum(i // tiles_per_seq, bp - 1), 0, j)),
                   pl.BlockSpec((nsq, 2, tf), lambda j, i: (jnp.maximum(i - n_ptiles, 0), 0, j))],
        out_shape=[jax.ShapeDtypeStruct((m, f), bf16),
                   jax.ShapeDtypeStruct((bp, 2, f), f32),
                   jax.ShapeDtypeStruct((c0s.shape[0], 2, f), f32)],
        scratch_shapes=[pltpu.VMEM((k, 2 * tf), bf16), pltpu.VMEM((2, tf), f32)],
        compiler_params=_params(("parallel", "arbitrary")),
        name="ffn_up_conv",
    )(xn, w_up, w_up, c0p, c0s, cw, cb.reshape(1, f))


def _first(accs, tiles, rows):
    return accs[0]


def _resid(accs, tiles, rows):
    return tiles[0] + accs[0]


def _ple(accs, tiles, rows):
    return tiles[0] + jax.nn.sigmoid(accs[0]) * accs[1]


def kernel(x_prompt, x_sample, p_prompt, p_sample, state_hgrn, state_rwkv, state_shift, state_ffn_conv, norm_mix, norm_ffn, norm_ple, norm_final, hg_w_in, hg_lb_logits, hg_gnorm, hg_w_o, rw_mu, rw_w_rkv, rw_w0, rw_w1, rw_w2, rw_a0, rw_a1, rw_a2, rw_g1, rw_g2, rw_k_k, rw_k_a, rw_r_k, rw_lnx_w, rw_lnx_b, rw_w_o, ffn_w_up, ffn_conv_w, ffn_conv_b, ffn_w_down, ple_w_proj, ple_w_gate):
    bp, tp, d = x_prompt.shape
    bs, ts, _ = x_sample.shape
    depth = norm_mix.shape[0]
    mp, ms = bp * tp, bs * ts
    groups = ((0, bp, tp), (mp, bs, ts))
    f_ff = ffn_conv_b.shape[1]
    w_down_bf = ffn_w_down.astype(bf16)

    h = jnp.concatenate([x_prompt.reshape(mp, d), x_sample.reshape(ms, d)], axis=0)
    p_all = jnp.concatenate([p_prompt.reshape(depth, mp, -1), p_sample.reshape(depth, ms, -1)], axis=1).astype(bf16)
    lb_all = jnp.cumsum(jax.nn.softmax(hg_lb_logits.astype(f32), axis=0), axis=0)

    hg_states = [[], []]
    rw_states = [[], []]
    sh_states = [[], []]
    cv_states = [[], []]
    for i in range(depth):
        j = i // 2
        if i % 2 == 0:
            xn = _rmsnorm(h, norm_mix[i], bf16)
            z = _mm([(xn, hg_w_in, j)], _first, f32, name="hgrn_in")
            mix_in = jnp.zeros((mp + ms, d), bf16)
            for gi, (off, nseq, t) in enumerate(groups):
                s0 = jnp.zeros((nseq, d // HG_HEAD, HG_HEAD, HG_HEAD), f32) if gi == 0 else state_hgrn[j].astype(f32)
                mix_in, s_new = _hgrn(z, lb_all[j], hg_gnorm[j], s0, off, nseq, t, mix_in)
                hg_states[gi].append(s_new)
            h = _mm([(mix_in, hg_w_o, j)], _resid, f32, tiles=(h,), name="hgrn_out")
        else:
            xr, xk, xv, *hidden = _norm_shift_mix(h, norm_mix[i], rw_mu[j], jnp.zeros((bp, d), f32),
                                                  state_shift[j].astype(f32), (rw_w1, rw_a1, rw_g1), j, mp, tp, ts)
            h_last = jnp.concatenate([h[tp - 1:mp:tp], h[mp + ts - 1::ts]], axis=0)
            xn_last = _rmsnorm(h_last, norm_mix[i], f32)
            sh_states[0].append(xn_last[:bp])
            sh_states[1].append(xn_last[bp:])
            r = _mm([(xr, rw_w_rkv, j)], _first, f32, ncols=d, name="rwkv_r")
            k = _mm([(xk, rw_w_rkv, j)], _first, f32, ncols=d, col_off=d, name="rwkv_k")
            v = _mm([(xv, rw_w_rkv, j)], _first, f32, ncols=d, col_off=2 * d, name="rwkv_v")
            vecs = (rw_k_k[j], rw_k_a[j], rw_r_k[j].reshape(d), rw_lnx_w[j], rw_lnx_b[j], rw_w0[j], rw_a0[j])
            mix_in = jnp.zeros((mp + ms, d), bf16)
            for gi, (off, nseq, t) in enumerate(groups):
                s0 = jnp.zeros((nseq, d // RW_HEAD, RW_HEAD, RW_HEAD), f32) if gi == 0 else state_rwkv[j].astype(f32)
                mix_in, s_new = _rwkv((r, k, v), hidden, (rw_w2, rw_a2, rw_g2), j, vecs, s0, off, nseq, t, mix_in)
                rw_states[gi].append(s_new)
            h = _mm([(mix_in, rw_w_o, j)], _resid, f32, tiles=(h,), name="rwkv_out")

        xn = _rmsnorm(h, norm_ffn[i], bf16)
        act, cn_p, cn_s = _ffn_up_conv(xn, ffn_w_up, i, jnp.zeros((bp, 2, f_ff), f32),
                                       state_ffn_conv[i].astype(f32), ffn_conv_w[i], ffn_conv_b[i], mp, tp, ts)
        cv_states[0].append(cn_p)
        cv_states[1].append(cn_s)
        h = _mm([(act, w_down_bf, i)], _resid, f32, tiles=(h,), tm=512, tn=512, stationary=False, name="ffn_down")

        xn = _rmsnorm(h, norm_ple[i], bf16)
        h = _mm([(xn, ple_w_gate, i), (p_all[i], ple_w_proj, i)], _ple, f32, tiles=(h,), name="ple")

    y_prompt = _rmsnorm(h, norm_final, f32, 0, mp).reshape(bp, tp, d)
    y_sample = _rmsnorm(h, norm_final, f32, mp, ms).reshape(bs, ts, d)
    st = lambda xs: jnp.stack(xs)
    return (y_prompt, y_sample,
            st(hg_states[0]), st(rw_states[0]), st(sh_states[0]), st(cv_states[0]),
            st(hg_states[1]), st(rw_states[1]), st(sh_states[1]), st(cv_states[1]))
```
